```python
import jax, jax.numpy as jnp
from jax import lax
import numpy as np

D_MODEL = 1024
BATCH = 4
SEQ = 8192
DEPTH = 2

D_MIX = D_MODEL
D_POOL = D_MIX // 2
D_GLA = D_MIX - D_POOL
POOL_WINDOWS = (2, 4, 8, 16)
N_POOL_GROUPS = len(POOL_WINDOWS)
POOL_GROUP = D_POOL // N_POOL_GROUPS
GLA_HEADS = 4
GLA_DV = D_GLA // GLA_HEADS
GLA_DK = GLA_DV // 2
GATE_RANK = 16
GATE_NORM = 16.0
CHUNK = 64
D_FF = ((8 * D_MODEL // 3 + 127) // 128) * 128
CONV_W = 3
EPS = 1e-6
IN_SIZES = (D_POOL, GLA_HEADS * GLA_DK, GLA_HEADS * GLA_DK, D_GLA, D_GLA, GATE_RANK)
D_IN = sum(IN_SIZES)
IN_SPLITS = [int(s) for s in np.cumsum(IN_SIZES)[:-1]]

kernel_name = "hybrid_pool_gla_convffn_adaln"


def rmsnorm(x):
    xf = x.astype(jnp.float32)
    xf = xf * lax.rsqrt(jnp.mean(xf * xf, axis=-1, keepdims=True) + EPS)
    return xf.astype(x.dtype)


def modulate(h, shift, scale):
    return h * (1 + scale[:, None, :]) + shift[:, None, :]


def pool_mixer(u, w_pool, pool_scale):
    B, S, _ = u.shape
    cs = jnp.cumsum(u.astype(jnp.float32), axis=1).reshape(B, S, N_POOL_GROUPS, POOL_GROUP)
    pos = jnp.arange(1, S + 1, dtype=jnp.float32)
    means = []
    for gi, w in enumerate(POOL_WINDOWS):
        cg = cs[:, :, gi]
        prev = jnp.pad(cg, ((0, 0), (w, 0), (0, 0)))[:, :S]
        cnt = jnp.minimum(pos, float(w))[None, :, None]
        means.append((cg - prev) / cnt)
    mean = jnp.stack(means, axis=2)
    d = (mean - u.reshape(B, S, N_POOL_GROUPS, POOL_GROUP).astype(jnp.float32)).astype(u.dtype)
    y = jnp.einsum('bsgc,gcd->bsgd', d, w_pool)
    return y.reshape(B, S, D_POOL) * pool_scale


def gla_chunk_step(state, inp):
    q, k, v, g = inp
    b = jnp.cumsum(g, axis=2)
    o_inter = jnp.einsum('bhcd,bhde->bhce', q * jnp.exp(b), state)
    diff = b[:, :, :, None, :] - b[:, :, None, :, :]
    mask = jnp.tril(jnp.ones((CHUNK, CHUNK), dtype=bool))[None, None, :, :, None]
    decay = jnp.where(mask, jnp.exp(jnp.minimum(diff, 0.0)), 0.0)
    attn = jnp.einsum('bhid,bhjd,bhijd->bhij', q, k, decay)
    o_intra = jnp.einsum('bhij,bhje->bhie', attn, v)
    b_last = b[:, :, -1, :]
    k_dec = k * jnp.exp(b_last[:, :, None, :] - b)
    new_state = state * jnp.exp(b_last)[..., None] + jnp.einsum('bhcd,bhce->bhde', k_dec, v)
    return new_state, o_inter + o_intra


def gla_mixer(q, k, v, r, z, w_gate2, b_gate, gla_norm):
    B, S, _ = q.shape
    n_chunks = S // CHUNK
    glog = jax.nn.log_sigmoid((z @ w_gate2 + b_gate).astype(jnp.float32)) / GATE_NORM

    def to_chunks(t, d):
        return t.reshape(B, n_chunks, CHUNK, GLA_HEADS, d).transpose(1, 0, 3, 2, 4).astype(jnp.float32)

    qc = to_chunks(q, GLA_DK) * (GLA_DK ** -0.5)
    kc = to_chunks(k, GLA_DK)
    vc = to_chunks(v, GLA_DV)
    gc = to_chunks(glog, GLA_DK)
    s0 = jnp.zeros((B, GLA_HEADS, GLA_DK, GLA_DV), jnp.float32)
    _, o = lax.scan(gla_chunk_step, s0, (qc, kc, vc, gc))
    o = o.transpose(1, 0, 3, 2, 4).reshape(B, S, GLA_HEADS, GLA_DV)
    o = o * lax.rsqrt(jnp.mean(o * o, axis=-1, keepdims=True) + EPS)
    o = o.reshape(B, S, D_GLA).astype(q.dtype) * gla_norm
    return o * jax.nn.silu(r)


def conv_ffn(h, w_up, conv_w, conv_b, w_down):
    up = h @ w_up
    ch = up.shape[-1]
    y = lax.conv_general_dilated(up, conv_w[:, None, :].astype(up.dtype), window_strides=(1,),
                                 padding=((CONV_W - 1, 0),),
                                 dimension_numbers=('NWC', 'WIO', 'NWC'),
                                 feature_group_count=ch) + conv_b
    a, b = jnp.split(y, 2, axis=-1)
    return (jax.nn.silu(a) * b) @ w_down


def setup_inputs(seed: int = 0) -> dict:
    key = jax.random.key(seed)
    ks = jax.random.split(key, 20)
    f32 = jnp.float32
    L = DEPTH

    def nrm(k, shape, scale):
        return jax.random.normal(k, shape, f32) * scale

    return {
        "x": nrm(ks[0], (BATCH, SEQ, D_MODEL), 1.0),
        "c": nrm(ks[1], (BATCH, D_MODEL), 1.0),
        "ada_w": nrm(ks[2], (L, D_MODEL, 6 * D_MODEL), 0.5 * D_MODEL ** -0.5),
        "ada_b": nrm(ks[3], (L, 6 * D_MODEL), 0.02),
        "w_in": nrm(ks[4], (L, D_MODEL, D_IN), D_MODEL ** -0.5),
        "w_gate2": nrm(ks[5], (L, GATE_RANK, GLA_HEADS * GLA_DK), GATE_RANK ** -0.5),
        "b_gate": nrm(ks[6], (L, GLA_HEADS * GLA_DK), 0.02),
        "w_pool": nrm(ks[7], (L, N_POOL_GROUPS, POOL_GROUP, POOL_GROUP), POOL_GROUP ** -0.5),
        "pool_scale": 1.0 + nrm(ks[8], (L, D_POOL), 0.02),
        "gla_norm": 1.0 + nrm(ks[9], (L, D_GLA), 0.02),
        "w_out": nrm(ks[10], (L, D_MIX, D_MODEL), D_MIX ** -0.5),
        "w_up": nrm(ks[11], (L, D_MODEL, 2 * D_FF), D_MODEL ** -0.5),
        "conv_w": nrm(ks[12], (L, CONV_W, 2 * D_FF), CONV_W ** -0.5),
        "conv_b": nrm(ks[13], (L, 2 * D_FF), 0.02),
        "w_down": nrm(ks[14], (L, D_FF, D_MODEL), D_FF ** -0.5),
        "final_norm": 1.0 + nrm(ks[15], (D_MODEL,), 0.02),
    }


def reference(x, c, ada_w, ada_b, w_in, w_gate2, b_gate, w_pool, pool_scale, gla_norm,
              w_out, w_up, conv_w, conv_b, w_down, final_norm):
    c_act = jax.nn.silu(c)
    for l in range(DEPTH):
        mod = c_act @ ada_w[l] + ada_b[l]
        sh1, sc1, g1, sh2, sc2, g2 = jnp.split(mod, 6, axis=-1)

        h = modulate(rmsnorm(x), sh1, sc1)
        u, q, k, v, r, z = jnp.split(h @ w_in[l], IN_SPLITS, axis=-1)
        y_pool = pool_mixer(u, w_pool[l], pool_scale[l])
        y_gla = gla_mixer(q, k, v, r, z, w_gate2[l], b_gate[l], gla_norm[l])
        y = jnp.concatenate([y_pool, y_gla], axis=-1) @ w_out[l]
        x = x + g1[:, None, :] * y

        h = modulate(rmsnorm(x), sh2, sc2)
        x = x + g2[:, None, :] * conv_ffn(h, w_up[l], conv_w[l], conv_b[l], w_down[l])
    return rmsnorm(x) * final_norm
```

```python
import functools

import jax
import jax.numpy as jnp
import numpy as np
from jax import lax
from jax.experimental import pallas as pl
from jax.experimental.pallas import tpu as pltpu

F32 = jnp.float32
BF16 = jnp.bfloat16

EPS = 1e-6
POOL_WINDOWS = (2, 4, 8, 16)
POOL_HIST = 16
GLA_HEADS = 4
GATE_RANK = 16
GATE_NORM = 16.0
CHUNK = 64
SUB = 8
OFF_LEVELS = (32, 16, 8)
CONV_W = 3
LANES = 128
VMEM_LIMIT = 56 * 1024 * 1024

TILE_MIX = 512
TILE_FFN = 512
FF_COLS = 256
MOD_COLS = 512


def _silu(x):
    return x * (1.0 / (1.0 + jnp.exp(-x)))


def _rms_mod(x, shift, scale):
    ms = jnp.mean(x * x, axis=-1, keepdims=True)
    return x * lax.rsqrt(ms + EPS) * (1.0 + scale) + shift


def _nt_dot(a, b):
    return lax.dot_general(a, b, (((1,), (1,)), ((), ())), preferred_element_type=F32)


def _mod_kernel(c_ref, w_ref, b_ref, o_ref):
    c = c_ref[...]
    ca = _silu(c)
    o_ref[0] = jnp.dot(ca, w_ref[0], preferred_element_type=F32,
                       precision=lax.Precision.HIGHEST) + b_ref[0]


def _modulation(c, ada_w, ada_b):
    n_layers, d, n = ada_w.shape
    bsz = c.shape[0]
    return pl.pallas_call(
        _mod_kernel,
        grid=(n_layers, n // MOD_COLS),
        in_specs=[
            pl.BlockSpec((bsz, d), lambda l, j: (0, 0)),
            pl.BlockSpec((1, d, MOD_COLS), lambda l, j: (l, 0, j)),
            pl.BlockSpec((1, 1, MOD_COLS), lambda l, j: (l, 0, j)),
        ],
        out_specs=pl.BlockSpec((1, bsz, MOD_COLS), lambda l, j: (l, 0, j)),
        out_shape=jax.ShapeDtypeStruct((n_layers, bsz, n), F32),
        compiler_params=pltpu.CompilerParams(dimension_semantics=("arbitrary", "arbitrary")),
        name="adaln_mod",
    )(c, ada_w, ada_b.reshape(n_layers, 1, n))


def _mixer_kernel(x_ref, mod_ref, w_in_ref, w_z_ref, w_g2_ref, b_gate_ref, w_pool_ref,
                  pool_scale_ref, gla_norm_ref, w_out_ref, tri_ref, wred_ref, hmean_ref,
                  o_ref,
                  uh_scr, s_scr, b_scr, q_scr, k_scr, v_scr, pc_scr, ad_scr, o_scr, bc_scr, kc_scr,
                  *, tile, d_model, d_pool, dk_all, dv_all):
    t = pl.program_id(1)
    n_chunks = tile // CHUNK
    n_pairs = dk_all // LANES
    dv_pair = dv_all // n_pairs

    @pl.when(t == 0)
    def _():
        uh_scr[0:POOL_HIST, :] = jnp.zeros((POOL_HIST, d_pool), F32)
        s_scr[...] = jnp.zeros(s_scr.shape, F32)

    x = x_ref[0]
    mod = mod_ref[0]
    sh1 = mod[:, 0:d_model]
    sc1 = mod[:, d_model:2 * d_model]
    g1 = mod[:, 2 * d_model:3 * d_model]
    hb = _rms_mod(x, sh1, sc1).astype(BF16)

    proj = jnp.dot(hb, w_in_ref[...], preferred_element_type=F32)
    z = jnp.dot(hb, w_z_ref[...], preferred_element_type=F32)
    o_q = d_pool
    o_k = o_q + dk_all
    o_v = o_k + dk_all
    o_r = o_v + dv_all

    uh_scr[POOL_HIST:POOL_HIST + tile, :] = proj[:, 0:d_pool]
    pos = (t * tile + 1 + lax.broadcasted_iota(jnp.int32, (tile, 1), 0)).astype(F32)
    y_pool = []
    for gi, w in enumerate(POOL_WINDOWS):
        cols = slice(gi * LANES, (gi + 1) * LANES)
        acc = uh_scr[POOL_HIST:POOL_HIST + tile, cols]
        u_g = acc
        for s in range(1, w):
            acc = acc + uh_scr[POOL_HIST - s:POOL_HIST - s + tile, cols]
        cnt = jnp.minimum(pos, float(w))
        d_g = (acc / cnt - u_g).astype(BF16)
        y_g = jnp.dot(d_g, w_pool_ref[gi], preferred_element_type=F32)
        y_pool.append(y_g * pool_scale_ref[:, cols])
    uh_scr[0:POOL_HIST, :] = uh_scr[tile:tile + POOL_HIST, :]

    gpre = jnp.dot(z.astype(BF16), w_g2_ref[...], preferred_element_type=F32) + b_gate_ref[...]
    glog = (jnp.minimum(gpre, 0.0) - jnp.log1p(jnp.exp(-jnp.abs(gpre)))) * (1.0 / GATE_NORM)
    g_hi = glog.astype(BF16)
    g_lo = (glog - g_hi.astype(F32)).astype(BF16)
    cs = jnp.dot(tri_ref[...], jnp.concatenate([g_hi, g_lo], axis=1), preferred_element_type=F32)
    b_scr[...] = cs[:, 0:dk_all] + cs[:, dk_all:2 * dk_all]
    q_scr[...] = proj[:, o_q:o_k] * (float(dk_all // GLA_HEADS) ** -0.5)
    k_scr[...] = proj[:, o_k:o_v]
    v_scr[...] = proj[:, o_v:o_r].astype(BF16)

    def pair_body(c, carry):
        base = pl.multiple_of(c * CHUNK, CHUNK)
        bc_scr[...] = b_scr[pl.ds(base, CHUNK), :]
        kc_scr[...] = k_scr[pl.ds(base, CHUNK), :]
        for hp in range(n_pairs):
            lanes = slice(hp * LANES, (hp + 1) * LANES)
            for blk2 in range(CHUNK // (2 * SUB)):
                rows = []
                for half in range(2):
                    l0 = (2 * blk2 + half) * SUB
                    b8 = bc_scr[l0:l0 + SUB, lanes]
                    q8 = q_scr[pl.ds(base + l0, SUB), lanes]
                    pj = []
                    for j in range(SUB):
                        bj = bc_scr[l0 + j:l0 + j + 1, lanes]
                        kj = kc_scr[l0 + j:l0 + j + 1, lanes]
                        e = jnp.exp(jnp.minimum(b8 - bj, 0.0))
                        pj.append(q8 * e * kj)
                    rows.append(jnp.concatenate(pj, axis=1))
                blk = jnp.concatenate(rows, axis=0).astype(BF16)
                pc_scr[hp, pl.ds(base + blk2 * 2 * SUB, 2 * SUB), :] = blk
        return carry

    lax.fori_loop(0, n_chunks, pair_body, 0)

    row_l = lax.broadcasted_iota(jnp.int32, (SUB, LANES), 0)
    col_l = lax.broadcasted_iota(jnp.int32, (SUB, LANES), 1) % (LANES // 2)
    tril8 = row_l >= col_l
    for hp in range(n_pairs):
        ad = jnp.dot(pc_scr[hp], wred_ref[...], preferred_element_type=F32)
        placed = []
        for rb in range(tile // SUB):
            a8 = jnp.where(tril8, ad[rb * SUB:(rb + 1) * SUB, :], 0.0)
            shift = (rb % (CHUNK // SUB)) * SUB
            placed.append(pltpu.roll(a8, shift, axis=1) if shift else a8)
        ad_scr[:, hp * LANES:(hp + 1) * LANES] = jnp.concatenate(placed, axis=0)

    ri = lax.broadcasted_iota(jnp.int32, (CHUNK, LANES), 0)
    ci = lax.broadcasted_iota(jnp.int32, (CHUNK, LANES), 1) % (LANES // 2)
    lane_lo = lax.broadcasted_iota(jnp.int32, (CHUNK, LANES), 1) < (LANES // 2)
    r2 = lax.broadcasted_iota(jnp.int32, (LANES, 2 * LANES), 0)
    c2 = lax.broadcasted_iota(jnp.int32, (LANES, 2 * LANES), 1)
    state_mask = (r2 < LANES // 2) == (c2 < LANES)

    def chunk_body(c, carry):
        base = pl.multiple_of(c * CHUNK, CHUNK)
        bc_scr[...] = b_scr[pl.ds(base, CHUNK), :]
        for hp in range(n_pairs):
            lanes = slice(hp * LANES, (hp + 1) * LANES)
            b2 = bc_scr[:, lanes]
            q2 = q_scr[pl.ds(base, CHUNK), lanes]
            k2 = k_scr[pl.ds(base, CHUNK), lanes]
            v2 = v_scr[pl.ds(base, CHUNK), hp * dv_pair:(hp + 1) * dv_pair]
            a_hp = ad_scr[pl.ds(base, CHUNK), lanes]
            for s in OFF_LEVELS:
                bref = jnp.concatenate(
                    [jnp.broadcast_to(bc_scr[((r // s) | 1) * s - 1:((r // s) | 1) * s, lanes], (s, LANES))
                     for r in range(0, CHUNK, s)], axis=0)
                odd = ((ri // s) % 2) == 1
                qs = jnp.where(odd, q2 * jnp.exp(jnp.minimum(b2 - bref, 0.0)), 0.0)
                ks = jnp.where(odd, 0.0, k2 * jnp.exp(jnp.minimum(bref - b2, 0.0)))
                kbd = jnp.concatenate([jnp.where(lane_lo, ks, 0.0), jnp.where(lane_lo, 0.0, ks)], axis=0)
                off = _nt_dot(qs.astype(BF16), kbd.astype(BF16))
                valid = odd & ((ci // s) == (ri // s) - 1)
                a_hp = a_hp + jnp.where(valid, off, 0.0)
            qe = q2 * jnp.exp(b2)
            zero_v = jnp.zeros((CHUNK, dv_pair // 2), BF16)
            vbd = jnp.concatenate(
                [jnp.concatenate([v2[:, 0:dv_pair // 2], zero_v], axis=1),
                 jnp.concatenate([zero_v, v2[:, dv_pair // 2:dv_pair]], axis=1)], axis=0)
            s_old = s_scr[hp]
            lhs = jnp.concatenate([qe, a_hp], axis=1).astype(BF16)
            rhs = jnp.concatenate([s_old.astype(BF16), vbd], axis=0)
            o_scr[pl.ds(base, CHUNK), hp * dv_pair:(hp + 1) * dv_pair] = jnp.dot(
                lhs, rhs, preferred_element_type=F32)
            b_last = bc_scr[CHUNK - 1:CHUNK, lanes]
            kdec = k2 * jnp.exp(jnp.minimum(b_last - b2, 0.0))
            kdec_t = jnp.transpose(jnp.concatenate([kdec, jnp.zeros_like(kdec)], axis=0))
            upd = jnp.dot(kdec_t[:, 0:CHUNK].astype(BF16), v2, preferred_element_type=F32)
            dcol = jnp.transpose(jnp.broadcast_to(jnp.exp(b_last), (LANES, LANES)))
            dcol2 = jnp.concatenate([dcol, dcol], axis=1)
            s_scr[hp] = s_old * dcol2 + jnp.where(state_mask, upd, 0.0)
        return carry

    lax.fori_loop(0, n_chunks, chunk_body, 0)

    o = o_scr[...]
    ms = jnp.dot((o * o).astype(BF16), hmean_ref[...], preferred_element_type=F32)
    y_gla = o * lax.rsqrt(ms + EPS) * gla_norm_ref[...] * _silu(proj[:, o_r:o_r + dv_all])
    y = jnp.concatenate(y_pool + [y_gla], axis=1).astype(BF16)
    o_ref[0] = x + g1 * jnp.dot(y, w_out_ref[...], preferred_element_type=F32)


def _const_spec(shape):
    zeros = (0,) * len(shape)
    return pl.BlockSpec(shape, lambda b, t: zeros, pipeline_mode=pl.Buffered(1))


def _mixer(x, mod, w_in, w_z, w_g2, b_gate, w_pool, pool_scale, gla_norm, w_out, tri, wred, hmean):
    bsz, seq, d_model = x.shape
    d_pool = pool_scale.shape[-1]
    dk_all = b_gate.shape[-1]
    dv_all = gla_norm.shape[-1]
    tile = min(TILE_MIX, seq)
    kern = functools.partial(_mixer_kernel, tile=tile, d_model=d_model, d_pool=d_pool,
                             dk_all=dk_all, dv_all=dv_all)
    n_pairs = dk_all // LANES
    consts = (w_in, w_z, w_g2, b_gate, w_pool, pool_scale, gla_norm, w_out, tri, wred, hmean)
    return pl.pallas_call(
        kern,
        grid=(bsz, seq // tile),
        in_specs=[pl.BlockSpec((1, tile, d_model), lambda b, t: (b, t, 0)),
                  pl.BlockSpec((1, 1, mod.shape[-1]), lambda b, t: (b, 0, 0))]
                 + [_const_spec(a.shape) for a in consts],
        out_specs=pl.BlockSpec((1, tile, d_model), lambda b, t: (b, t, 0)),
        out_shape=jax.ShapeDtypeStruct(x.shape, F32),
        scratch_shapes=[
            pltpu.VMEM((POOL_HIST + tile, d_pool), F32),
            pltpu.VMEM((n_pairs, LANES, dv_all // n_pairs), F32),
            pltpu.VMEM((tile, dk_all), F32),
            pltpu.VMEM((tile, dk_all), F32),
            pltpu.VMEM((tile, dk_all), F32),
            pltpu.VMEM((tile, dv_all), BF16),
            pltpu.VMEM((n_pairs, tile, SUB * LANES), BF16),
            pltpu.VMEM((tile, dk_all), F32),
            pltpu.VMEM((tile, dv_all), F32),
            pltpu.VMEM((CHUNK, dk_all), F32),
            pltpu.VMEM((CHUNK, dk_all), F32),
        ],
        compiler_params=pltpu.CompilerParams(dimension_semantics=("arbitrary", "arbitrary"),
                                             vmem_limit_bytes=VMEM_LIMIT),
        name="token_mixer",
    )(x, mod, *consts)


def _ffn_kernel(x_ref, mod_ref, w_up_ref, cw_ref, cb_ref, w_dn_ref, fn_ref, o_ref,
                carry_scr, ub_scr, *, tile, d_model, d_ff, final):
    t = pl.program_id(1)

    @pl.when(t == 0)
    def _():
        carry_scr[...] = jnp.zeros(carry_scr.shape, F32)

    x = x_ref[0]
    mod = mod_ref[0]
    sh2 = mod[:, 3 * d_model:4 * d_model]
    sc2 = mod[:, 4 * d_model:5 * d_model]
    g2 = mod[:, 5 * d_model:6 * d_model]
    hb = _rms_mod(x, sh2, sc2).astype(BF16)

    acc = jnp.zeros((tile, d_model), F32)
    for c in range(d_ff // FF_COLS):
        halves = []
        for half in range(2):
            cols = slice(half * d_ff + c * FF_COLS, half * d_ff + (c + 1) * FF_COLS)
            ucol = slice(half * FF_COLS, (half + 1) * FF_COLS)
            up = jnp.dot(hb, w_up_ref[:, cols], preferred_element_type=F32)
            ub_scr[8:8 + tile, ucol] = up
            ub_scr[0:8, ucol] = carry_scr[:, cols]
            carry_scr[:, cols] = up[tile - 8:tile, :]
            up1 = ub_scr[7:7 + tile, ucol]
            up2 = ub_scr[6:6 + tile, ucol]
            halves.append(cw_ref[2:3, cols] * up + cw_ref[1:2, cols] * up1
                          + cw_ref[0:1, cols] * up2 + cb_ref[:, cols])
        act = (_silu(halves[0]) * halves[1]).astype(BF16)
        acc = acc + jnp.dot(act, w_dn_ref[c * FF_COLS:(c + 1) * FF_COLS, :],
                            preferred_element_type=F32)
    out = x + g2 * acc
    if final:
        ms = jnp.mean(out * out, axis=-1, keepdims=True)
        out = out * lax.rsqrt(ms + EPS) * fn_ref[...]
    o_ref[0] = out


def _ffn(x, mod, w_up, conv_w, conv_b, w_down, final_norm, *, final):
    bsz, seq, d_model = x.shape
    d_ff = w_down.shape[0]
    tile = min(TILE_FFN, seq)
    kern = functools.partial(_ffn_kernel, tile=tile, d_model=d_model, d_ff=d_ff, final=final)
    consts = (w_up, conv_w, conv_b, w_down, final_norm)
    return pl.pallas_call(
        kern,
        grid=(bsz, seq // tile),
        in_specs=[pl.BlockSpec((1, tile, d_model), lambda b, t: (b, t, 0)),
                  pl.BlockSpec((1, 1, mod.shape[-1]), lambda b, t: (b, 0, 0))]
                 + [_const_spec(a.shape) for a in consts],
        out_specs=pl.BlockSpec((1, tile, d_model), lambda b, t: (b, t, 0)),
        out_shape=jax.ShapeDtypeStruct(x.shape, F32),
        scratch_shapes=[
            pltpu.VMEM((8, 2 * d_ff), F32),
            pltpu.VMEM((8 + tile, 2 * FF_COLS), F32),
        ],
        compiler_params=pltpu.CompilerParams(dimension_semantics=("arbitrary", "arbitrary"),
                                             vmem_limit_bytes=VMEM_LIMIT),
        name="conv_ffn",
    )(x, mod, *consts)


def _mixer_constants(tile, dk_all, dv_all):
    r = np.arange(tile)
    tri = ((r[:, None] // CHUNK == r[None, :] // CHUNK) & (r[:, None] >= r[None, :]))
    kk = np.arange(SUB * LANES)
    wred = (kk[:, None] // LANES + (LANES // 2) * ((kk[:, None] % LANES) // (LANES // 2))
            == np.arange(LANES)[None, :])
    dv = dv_all // GLA_HEADS
    e = np.arange(dv_all)
    hmean = (e[:, None] // dv == e[None, :] // dv) / float(dv)
    return (jnp.asarray(tri, BF16), jnp.asarray(wred, BF16), jnp.asarray(hmean, BF16))


def kernel(x, c, ada_w, ada_b, w_in, w_gate2, b_gate, w_pool, pool_scale, gla_norm, w_out, w_up,
           conv_w, conv_b, w_down, final_norm):
    depth = ada_w.shape[0]
    seq = x.shape[1]
    d_model = x.shape[2]
    d_pool = pool_scale.shape[-1]
    dk_all = b_gate.shape[-1]
    dv_all = gla_norm.shape[-1]
    n_main = d_pool + 2 * dk_all + 2 * dv_all
    assert w_in.shape[-1] == n_main + GATE_RANK and dk_all == 2 * LANES and seq % CHUNK == 0

    mods = _modulation(c, ada_w, ada_b)
    tri, wred, hmean = _mixer_constants(min(TILE_MIX, seq), dk_all, dv_all)
    pad_z = LANES - GATE_RANK
    for l in range(depth):
        mod = mods[l][:, None, :]
        w_z = jnp.pad(w_in[l][:, n_main:], ((0, 0), (0, pad_z))).astype(BF16)
        w_g2 = jnp.pad(w_gate2[l], ((0, pad_z), (0, 0))).astype(BF16)
        x = _mixer(x, mod, w_in[l][:, :n_main].astype(BF16), w_z, w_g2, b_gate[l][None, :],
                   w_pool[l].astype(BF16), pool_scale[l][None, :], gla_norm[l][None, :],
                   w_out[l].astype(BF16), tri, wred, hmean)
        x = _ffn(x, mod, w_up[l].astype(BF16), conv_w[l], conv_b[l][None, :],
                 w_down[l].astype(BF16), final_norm[None, :], final=(l == depth - 1))
    return x
```

```python
import functools

import jax
import jax.numpy as jnp
import numpy as np
from jax import lax
from jax.experimental import pallas as pl
from jax.experimental.pallas import tpu as pltpu

F32 = jnp.float32
BF16 = jnp.bfloat16

EPS = 1e-6
POOL_WINDOWS = (2, 4, 8, 16)
POOL_HIST = 16
GLA_HEADS = 4
GATE_RANK = 16
GATE_NORM = 16.0
CHUNK = 64
SUB = 8
OFF_LEVELS = (32, 16, 8)
CONV_W = 3
LANES = 128
VMEM_LIMIT = 56 * 1024 * 1024

TILE_MIX = 512
TILE_FFN = 512
FF_COLS = 256
MOD_COLS = 512


def _silu(x):
    return x * (1.0 / (1.0 + jnp.exp(-x)))


def _rms_mod(x, shift, scale):
    ms = jnp.mean(x * x, axis=-1, keepdims=True)
    return x * lax.rsqrt(ms + EPS) * (1.0 + scale) + shift


def _nt_dot(a, b):
    return lax.dot_general(a, b, (((1,), (1,)), ((), ())), preferred_element_type=F32)


def _mod_kernel(c_ref, w_ref, b_ref, o_ref):
    c = c_ref[...]
    ca = _silu(c)
    o_ref[0] = jnp.dot(ca, w_ref[0], preferred_element_type=F32,
                       precision=lax.Precision.HIGHEST) + b_ref[0]


def _modulation(c, ada_w, ada_b):
    n_layers, d, n = ada_w.shape
    bsz = c.shape[0]
    return pl.pallas_call(
        _mod_kernel,
        grid=(n_layers, n // MOD_COLS),
        in_specs=[
            pl.BlockSpec((bsz, d), lambda l, j: (0, 0)),
            pl.BlockSpec((1, d, MOD_COLS), lambda l, j: (l, 0, j)),
            pl.BlockSpec((1, 1, MOD_COLS), lambda l, j: (l, 0, j)),
        ],
        out_specs=pl.BlockSpec((1, bsz, MOD_COLS), lambda l, j: (l, 0, j)),
        out_shape=jax.ShapeDtypeStruct((n_layers, bsz, n), F32),
        compiler_params=pltpu.CompilerParams(dimension_semantics=("arbitrary", "arbitrary")),
        name="adaln_mod",
    )(c, ada_w, ada_b.reshape(n_layers, 1, n))


def _mixer_kernel(x_ref, mod_ref, w_in_ref, w_z_ref, w_g2_ref, b_gate_ref, w_pool_ref,
                  pool_scale_ref, gla_norm_ref, w_out_ref, tri_ref, wred_ref, hmean_ref,
                  o_ref,
                  uh_scr, s_scr, b_scr, q_scr, k_scr, v_scr, pc_scr, ad_scr, o_scr,
                  *, tile, d_model, d_pool, dk_all, dv_all):
    t = pl.program_id(1)
    n_chunks = tile // CHUNK
    n_pairs = dk_all // LANES
    dv_pair = dv_all // n_pairs

    @pl.when(t == 0)
    def _():
        uh_scr[0:POOL_HIST, :] = jnp.zeros((POOL_HIST, d_pool), F32)
        s_scr[...] = jnp.zeros(s_scr.shape, F32)

    x = x_ref[0]
    mod = mod_ref[0]
    sh1 = mod[:, 0:d_model]
    sc1 = mod[:, d_model:2 * d_model]
    g1 = mod[:, 2 * d_model:3 * d_model]
    hb = _rms_mod(x, sh1, sc1).astype(BF16)

    proj = jnp.dot(hb, w_in_ref[...], preferred_element_type=F32)
    z = jnp.dot(hb, w_z_ref[...], preferred_element_type=F32)
    o_q = d_pool
    o_k = o_q + dk_all
    o_v = o_k + dk_all
    o_r = o_v + dv_all

    uh_scr[POOL_HIST:POOL_HIST + tile, :] = proj[:, 0:d_pool]
    pos = (t * tile + 1 + lax.broadcasted_iota(jnp.int32, (tile, 1), 0)).astype(F32)
    y_pool = []
    for gi, w in enumerate(POOL_WINDOWS):
        cols = slice(gi * LANES, (gi + 1) * LANES)
        acc = uh_scr[POOL_HIST:POOL_HIST + tile, cols]
        u_g = acc
        for s in range(1, w):
            acc = acc + uh_scr[POOL_HIST - s:POOL_HIST - s + tile, cols]
        cnt = jnp.minimum(pos, float(w))
        d_g = (acc / cnt - u_g).astype(BF16)
        y_g = jnp.dot(d_g, w_pool_ref[gi], preferred_element_type=F32)
        y_pool.append(y_g * pool_scale_ref[:, cols])
    uh_scr[0:POOL_HIST, :] = uh_scr[tile:tile + POOL_HIST, :]

    gpre = jnp.dot(z.astype(BF16), w_g2_ref[...], preferred_element_type=F32) + b_gate_ref[...]
    glog = (jnp.minimum(gpre, 0.0) - jnp.log1p(jnp.exp(-jnp.abs(gpre)))) * (1.0 / GATE_NORM)
    g_hi = glog.astype(BF16)
    g_lo = (glog - g_hi.astype(F32)).astype(BF16)
    cs = jnp.dot(tri_ref[...], jnp.concatenate([g_hi, g_lo], axis=1), preferred_element_type=F32)
    b_scr[...] = cs[:, 0:dk_all] + cs[:, dk_all:2 * dk_all]
    q_scr[...] = proj[:, o_q:o_k] * (float(dk_all // GLA_HEADS) ** -0.5)
    k_scr[...] = proj[:, o_k:o_v]
    v_scr[...] = proj[:, o_v:o_r].astype(BF16)

    for c in range(n_chunks):
        for hp in range(n_pairs):
            lanes = slice(hp * LANES, (hp + 1) * LANES)
            for blk2 in range(CHUNK // (2 * SUB)):
                rows = []
                for half in range(2):
                    r0 = c * CHUNK + (2 * blk2 + half) * SUB
                    b8 = b_scr[r0:r0 + SUB, lanes]
                    q8 = q_scr[r0:r0 + SUB, lanes]
                    pj = []
                    for j in range(SUB):
                        bj = b_scr[r0 + j:r0 + j + 1, lanes]
                        kj = k_scr[r0 + j:r0 + j + 1, lanes]
                        e = jnp.exp(jnp.minimum(b8 - bj, 0.0))
                        pj.append(q8 * e * kj)
                    rows.append(jnp.concatenate(pj, axis=1))
                blk = jnp.concatenate(rows, axis=0).astype(BF16)
                r0 = c * CHUNK + blk2 * 2 * SUB
                pc_scr[hp, r0:r0 + 2 * SUB, :] = blk

    row_l = lax.broadcasted_iota(jnp.int32, (SUB, LANES), 0)
    col_l = lax.broadcasted_iota(jnp.int32, (SUB, LANES), 1) % (LANES // 2)
    tril8 = row_l >= col_l
    for hp in range(n_pairs):
        ad = jnp.dot(pc_scr[hp], wred_ref[...], preferred_element_type=F32)
        placed = []
        for rb in range(tile // SUB):
            a8 = jnp.where(tril8, ad[rb * SUB:(rb + 1) * SUB, :], 0.0)
            shift = (rb % (CHUNK // SUB)) * SUB
            placed.append(pltpu.roll(a8, shift, axis=1) if shift else a8)
        ad_scr[:, hp * LANES:(hp + 1) * LANES] = jnp.concatenate(placed, axis=0)

    ri =lax.broadcasted_iota(jnp.int32, (CHUNK, LANES), 0)
    ci = lax.broadcasted_iota(jnp.int32, (CHUNK, LANES), 1) % (LANES // 2)
    lane_lo = lax.broadcasted_iota(jnp.int32, (CHUNK, LANES), 1) < (LANES // 2)
    r2 = lax.broadcasted_iota(jnp.int32, (LANES, 2 * LANES), 0)
    c2 = lax.broadcasted_iota(jnp.int32, (LANES, 2 * LANES), 1)
    state_mask = (r2 < LANES // 2) == (c2 < LANES)

    zero_v = jnp.zeros((CHUNK, dv_pair // 2), BF16)
    state = [s_scr[hp] for hp in range(n_pairs)]
    for c in range(n_chunks):
        rows = slice(c * CHUNK, (c + 1) * CHUNK)
        for hp in range(n_pairs):
            lanes = slice(hp * LANES, (hp + 1) * LANES)
            b2 = b_scr[rows, lanes]
            q2 = q_scr[rows, lanes]
            k2 = k_scr[rows, lanes]
            v2 = v_scr[rows, hp * dv_pair:(hp + 1) * dv_pair]
            a_hp = ad_scr[rows, lanes]
            for s in OFF_LEVELS:
                refs = [c * CHUNK + ((r // s) | 1) * s - 1 for r in range(0, CHUNK, s)]
                bref = jnp.concatenate(
                    [jnp.broadcast_to(b_scr[r:r + 1, lanes], (s, LANES)) for r in refs], axis=0)
                odd = ((ri // s) % 2) == 1
                qs = jnp.where(odd, q2 * jnp.exp(jnp.minimum(b2 - bref, 0.0)), 0.0)
                ks = jnp.where(odd, 0.0, k2 * jnp.exp(jnp.minimum(bref - b2, 0.0)))
                kbd = jnp.concatenate([jnp.where(lane_lo, ks, 0.0), jnp.where(lane_lo, 0.0, ks)], axis=0)
                off = _nt_dot(qs.astype(BF16), kbd.astype(BF16))
                valid = odd & ((ci // s) == (ri // s) - 1)
                a_hp = a_hp + jnp.where(valid, off, 0.0)
            qe = q2 * jnp.exp(b2)
            vbd = jnp.concatenate(
                [jnp.concatenate([v2[:, 0:dv_pair // 2], zero_v], axis=1),
                 jnp.concatenate([zero_v, v2[:, dv_pair // 2:dv_pair]], axis=1)], axis=0)
            lhs = jnp.concatenate([qe, a_hp], axis=1).astype(BF16)
            rhs = jnp.concatenate([state[hp].astype(BF16), vbd], axis=0)
            o_scr[rows, hp * dv_pair:(hp + 1) * dv_pair] = jnp.dot(
                lhs, rhs, preferred_element_type=F32)
            b_last = b_scr[(c + 1) * CHUNK - 1:(c + 1) * CHUNK, lanes]
            kdec = k2 * jnp.exp(jnp.minimum(b_last - b2, 0.0))
            kdec_t = jnp.transpose(jnp.concatenate([kdec, jnp.zeros_like(kdec)], axis=0))
            upd = jnp.dot(kdec_t[:, 0:CHUNK].astype(BF16), v2, preferred_element_type=F32)
            dcol = jnp.transpose(jnp.broadcast_to(jnp.exp(b_last), (LANES, LANES)))
            dcol2 = jnp.concatenate([dcol, dcol], axis=1)
            state[hp] = state[hp] * dcol2 + jnp.where(state_mask, upd, 0.0)
    for hp in range(n_pairs):
        s_scr[hp] = state[hp]

    o = o_scr[...]
    ms = jnp.dot((o * o).astype(BF16), hmean_ref[...], preferred_element_type=F32)
    y_gla = o * lax.rsqrt(ms + EPS) * gla_norm_ref[...] * _silu(proj[:, o_r:o_r + dv_all])
    y = jnp.concatenate(y_pool + [y_gla], axis=1).astype(BF16)
    o_ref[0] = x + g1 * jnp.dot(y, w_out_ref[...], preferred_element_type=F32)


def _const_spec(shape):
    zeros = (0,) * len(shape)
    return pl.BlockSpec(shape, lambda b, t: zeros, pipeline_mode=pl.Buffered(1))


def _mixer(x, mod, w_in, w_z, w_g2, b_gate, w_pool, pool_scale, gla_norm, w_out, tri, wred, hmean):
    bsz, seq, d_model = x.shape
    d_pool = pool_scale.shape[-1]
    dk_all = b_gate.shape[-1]
    dv_all = gla_norm.shape[-1]
    tile = min(TILE_MIX, seq)
    kern = functools.partial(_mixer_kernel, tile=tile, d_model=d_model, d_pool=d_pool,
                             dk_all=dk_all, dv_all=dv_all)
    n_pairs = dk_all // LANES
    consts = (w_in, w_z, w_g2, b_gate, w_pool, pool_scale, gla_norm, w_out, tri, wred, hmean)
    return pl.pallas_call(
        kern,
        grid=(bsz, seq // tile),
        in_specs=[pl.BlockSpec((1, tile, d_model), lambda b, t: (b, t, 0)),
                  pl.BlockSpec((1, 1, mod.shape[-1]), lambda b, t: (b, 0, 0))]
                 + [_const_spec(a.shape) for a in consts],
        out_specs=pl.BlockSpec((1, tile, d_model), lambda b, t: (b, t, 0)),
        out_shape=jax.ShapeDtypeStruct(x.shape, F32),
        scratch_shapes=[
            pltpu.VMEM((POOL_HIST + tile, d_pool), F32),
            pltpu.VMEM((n_pairs, LANES, dv_all // n_pairs), F32),
            pltpu.VMEM((tile, dk_all), F32),
            pltpu.VMEM((tile, dk_all), F32),
            pltpu.VMEM((tile, dk_all), F32),
            pltpu.VMEM((tile, dv_all), BF16),
            pltpu.VMEM((n_pairs, tile, SUB * LANES), BF16),
            pltpu.VMEM((tile, dk_all), F32),
            pltpu.VMEM((tile, dv_all), F32),
        ],
        compiler_params=pltpu.CompilerParams(dimension_semantics=("arbitrary", "arbitrary"),
                                             vmem_limit_bytes=VMEM_LIMIT),
        name="token_mixer",
    )(x, mod, *consts)


def _ffn_kernel(x_ref, mod_ref, w_up_ref, cw_ref, cb_ref, w_dn_ref, fn_ref, o_ref,
                carry_scr, act_scr, *, tile, d_model, d_ff, final):
    t = pl.program_id(1)

    @pl.when(t == 0)
    def _():
        carry_scr[...] = jnp.zeros(carry_scr.shape, F32)

    x = x_ref[0]
    mod = mod_ref[0]
    sh2 = mod[:, 3 * d_model:4 * d_model]
    sc2 = mod[:, 4 * d_model:5 * d_model]
    g2 = mod[:, 5 * d_model:6 * d_model]
    hb = _rms_mod(x, sh2, sc2).astype(BF16)

    row8 = lax.broadcasted_iota(jnp.int32, (8, FF_COLS), 0)
    for c in range(d_ff // FF_COLS):
        halves = []
        for half in range(2):
            cols = slice(half * d_ff + c * FF_COLS, half * d_ff + (c + 1) * FF_COLS)
            up = jnp.dot(hb, w_up_ref[:, cols], preferred_element_type=F32)
            prev = carry_scr[:, cols]
            carry_scr[:, cols] = up[tile - 8:tile, :]
            r1 = pltpu.roll(up, 1, axis=0)
            r2 = pltpu.roll(up, 2, axis=0)
            h1 = jnp.where(row8 < 1, pltpu.roll(prev, 1, axis=0), r1[0:8, :])
            h2 = jnp.where(row8 < 2, pltpu.roll(prev, 2, axis=0), r2[0:8, :])
            up1 = jnp.concatenate([h1, r1[8:, :]], axis=0)
            up2 = jnp.concatenate([h2, r2[8:, :]], axis=0)
            halves.append(cw_ref[2:3, cols] * up + cw_ref[1:2, cols] * up1
                          + cw_ref[0:1, cols] * up2 + cb_ref[:, cols])
        act_scr[:, c * FF_COLS:(c + 1) * FF_COLS] = (_silu(halves[0]) * halves[1]).astype(BF16)
    acc = jnp.dot(act_scr[...], w_dn_ref[...], preferred_element_type=F32)
    out = x + g2 * acc
    if final:
        ms = jnp.mean(out * out, axis=-1, keepdims=True)
        out = out * lax.rsqrt(ms + EPS) * fn_ref[...]
    o_ref[0] = out


def _ffn(x, mod, w_up, conv_w, conv_b, w_down, final_norm, *, final):
    bsz, seq, d_model = x.shape
    d_ff = w_down.shape[0]
    tile = min(TILE_FFN, seq)
    kern = functools.partial(_ffn_kernel, tile=tile, d_model=d_model, d_ff=d_ff, final=final)
    consts = (w_up, conv_w, conv_b, w_down, final_norm)
    return pl.pallas_call(
        kern,
        grid=(bsz, seq // tile),
        in_specs=[pl.BlockSpec((1, tile, d_model), lambda b, t: (b, t, 0)),
                  pl.BlockSpec((1, 1, mod.shape[-1]), lambda b, t: (b, 0, 0))]
                 + [_const_spec(a.shape) for a in consts],
        out_specs=pl.BlockSpec((1, tile, d_model), lambda b, t: (b, t, 0)),
        out_shape=jax.ShapeDtypeStruct(x.shape, F32),
        scratch_shapes=[
            pltpu.VMEM((8, 2 * d_ff), F32),
            pltpu.VMEM((tile, d_ff), BF16),
        ],
        compiler_params=pltpu.CompilerParams(dimension_semantics=("arbitrary", "arbitrary"),
                                             vmem_limit_bytes=VMEM_LIMIT),
        name="conv_ffn",
    )(x, mod, *consts)


def _mixer_constants(tile, dk_all, dv_all):
    r = np.arange(tile)
    tri = ((r[:, None] // CHUNK == r[None, :] // CHUNK) & (r[:, None] >= r[None, :]))
    kk = np.arange(SUB * LANES)
    wred = (kk[:, None] // LANES + (LANES // 2) * ((kk[:, None] % LANES) // (LANES // 2))
            == np.arange(LANES)[None, :])
    dv = dv_all // GLA_HEADS
    e = np.arange(dv_all)
    hmean = (e[:, None] // dv == e[None, :] // dv) / float(dv)
    return (jnp.asarray(tri, BF16), jnp.asarray(wred, BF16), jnp.asarray(hmean, BF16))


def kernel(x, c, ada_w, ada_b, w_in, w_gate2, b_gate, w_pool, pool_scale, gla_norm, w_out, w_up,
           conv_w, conv_b, w_down, final_norm):
    depth = ada_w.shape[0]
    seq = x.shape[1]
    d_model = x.shape[2]
    d_pool = pool_scale.shape[-1]
    dk_all = b_gate.shape[-1]
    dv_all = gla_norm.shape[-1]
    n_main = d_pool + 2 * dk_all + 2 * dv_all
    assert w_in.shape[-1] == n_main + GATE_RANK and dk_all == 2 * LANES and seq % CHUNK == 0

    mods = _modulation(c, ada_w, ada_b)
    tri, wred, hmean = _mixer_constants(min(TILE_MIX, seq), dk_all, dv_all)
    pad_z = LANES - GATE_RANK
    for l in range(depth):
        mod = mods[l][:, None, :]
        w_z = jnp.pad(w_in[l][:, n_main:], ((0, 0), (0, pad_z))).astype(BF16)
        w_g2 = jnp.pad(w_gate2[l], ((0, pad_z), (0, 0))).astype(BF16)
        x = _mixer(x, mod, w_in[l][:, :n_main].astype(BF16), w_z, w_g2, b_gate[l][None, :],
                   w_pool[l].astype(BF16), pool_scale[l][None, :], gla_norm[l][None, :],
                   w_out[l].astype(BF16), tri, wred, hmean)
        x = _ffn(x, mod, w_up[l].astype(BF16), conv_w[l], conv_b[l][None, :],
                 w_down[l].astype(BF16), final_norm[None, :], final=(l == depth - 1))
    return x
```

```python
import functools

import jax
import jax.numpy as jnp
import numpy as np
from jax import lax
from jax.experimental import pallas as pl
from jax.experimental.pallas import tpu as pltpu

F32 = jnp.float32
BF16 = jnp.bfloat16

EPS = 1e-6
POOL_WINDOWS = (2, 4, 8, 16)
POOL_HIST = 16
GLA_HEADS = 4
GATE_RANK = 16
GATE_NORM = 16.0
CHUNK = 64
SUB = 8
OFF_LEVELS = (32, 16, 8)
CONV_W = 3
LANES = 128
VMEM_LIMIT = 56 * 1024 * 1024

TILE_MIX = 512
TILE_FFN = 512
FF_COLS = 256
MOD_COLS = 512


def _silu(x):
    return x * (1.0 / (1.0 + jnp.exp(-x)))


def _rms_mod(x, shift, scale):
    ms = jnp.mean(x * x, axis=-1, keepdims=True)
    return x * lax.rsqrt(ms + EPS) * (1.0 + scale) + shift


def _nt_dot(a, b):
    return lax.dot_general(a, b, (((1,), (1,)), ((), ())), preferred_element_type=F32)


def _mod_kernel(c_ref, w_ref, b_ref, o_ref):
    c = c_ref[...]
    ca = _silu(c)
    o_ref[0] = jnp.dot(ca, w_ref[0], preferred_element_type=F32,
                       precision=lax.Precision.HIGHEST) + b_ref[0]


def _modulation(c, ada_w, ada_b):
    n_layers, d, n = ada_w.shape
    bsz = c.shape[0]
    return pl.pallas_call(
        _mod_kernel,
        grid=(n_layers, n // MOD_COLS),
        in_specs=[
            pl.BlockSpec((bsz, d), lambda l, j: (0, 0)),
            pl.BlockSpec((1, d, MOD_COLS), lambda l, j: (l, 0, j)),
            pl.BlockSpec((1, 1, MOD_COLS), lambda l, j: (l, 0, j)),
        ],
        out_specs=pl.BlockSpec((1, bsz, MOD_COLS), lambda l, j: (l, 0, j)),
        out_shape=jax.ShapeDtypeStruct((n_layers, bsz, n), F32),
        compiler_params=pltpu.CompilerParams(dimension_semantics=("arbitrary", "arbitrary")),
        name="adaln_mod",
    )(c, ada_w, ada_b.reshape(n_layers, 1, n))


def _tile_specs(bsz, n_t, tile, d_model, mod_width, layer):
    last = bsz * n_t - 1

    def cur(i):
        return jnp.minimum(i, last)

    def prev(i):
        return jnp.maximum(i - 1, 0)

    x_cur = pl.BlockSpec((1, tile, d_model), lambda i: (cur(i) // n_t, cur(i) % n_t, 0))
    x_prev = pl.BlockSpec((1, tile, d_model), lambda i: (prev(i) // n_t, prev(i) % n_t, 0))
    mod_cur = pl.BlockSpec((None, 1, 1, mod_width), lambda i: (layer, cur(i) // n_t, 0, 0))
    return x_cur, x_prev, mod_cur


def _const_spec(shape):
    zeros = (0,) * len(shape)
    return pl.BlockSpec(shape, lambda i: zeros, pipeline_mode=pl.Buffered(1))


def _layer_spec(shape, layer):
    idx = (layer,) + (0,) * (len(shape) - 1)
    return pl.BlockSpec((None,) + tuple(shape[1:]), lambda i: idx, pipeline_mode=pl.Buffered(1))


def _mixer_kernel(x_ref, mod_ref, w_in_ref, w_z_ref, w_g2_ref, b_gate_ref,
                  w_pool_ref, pool_scale_ref, gla_norm_ref, w_out_ref, tri_ref, wred_ref, hmean_ref,
                  o_ref,
                  uh_scr, s_scr, ad_scr, o_scr, lhs_scr, dcol_scr, upd_scr,
                  b0_scr, q0_scr, k0_scr, v0_scr, pc0_scr, yp0_scr, r0_scr, x0_scr, mod0_scr, hist0_scr,
                  b1_scr, q1_scr, k1_scr, v1_scr, pc1_scr, yp1_scr, r1_scr, x1_scr, mod1_scr, hist1_scr,
                  *, tile, n_t, n_tiles, d_model, d_pool, dk_all, dv_all):
    i = pl.program_id(0)
    t_cur = jnp.minimum(i, n_tiles - 1) % n_t
    t_prev = jnp.maximum(i - 1, 0) % n_t
    n_chunks = tile // CHUNK
    n_pairs = dk_all // LANES
    dv_pair = dv_all // n_pairs
    set0 = (b0_scr, q0_scr, k0_scr, v0_scr, pc0_scr, yp0_scr, r0_scr, x0_scr, mod0_scr, hist0_scr)
    set1 = (b1_scr, q1_scr, k1_scr, v1_scr, pc1_scr, yp1_scr, r1_scr, x1_scr, mod1_scr, hist1_scr)

    def step(cur, prv):
        b_scr, q_scr, k_scr, v_scr, pc_scr, yp_scr, r_scr, xc_scr, modc_scr, hist_scr = cur
        bp_scr, qp_scr, kp_scr, vp_scr, pcp_scr, ypp_scr, rp_scr, xp_scr, modp_scr, histp_scr = prv
        o_q = d_pool
        o_k = o_q + dk_all
        o_v = o_k + dk_all
        o_r = o_v + dv_all

        row_l = lax.broadcasted_iota(jnp.int32, (SUB, LANES), 0)
        col_l = lax.broadcasted_iota(jnp.int32, (SUB, LANES), 1) % (LANES // 2)
        tril8 = row_l >= col_l
        for hp in range(n_pairs):
            ad = jnp.dot(pcp_scr[hp], wred_ref[...], preferred_element_type=F32)
            placed = []
            for rb in range(tile // SUB):
                a8 = jnp.where(tril8, ad[rb * SUB:(rb + 1) * SUB, :], 0.0)
                shift = (rb % (CHUNK // SUB)) * SUB
                placed.append(pltpu.roll(a8, shift, axis=1) if shift else a8)
            ad_scr[:, hp * LANES:(hp + 1) * LANES] = jnp.concatenate(placed, axis=0)

        ri = lax.broadcasted_iota(jnp.int32, (CHUNK, LANES), 0)
        ci = lax.broadcasted_iota(jnp.int32, (CHUNK, LANES), 1) % (LANES // 2)
        lane_lo = lax.broadcasted_iota(jnp.int32, (CHUNK, LANES), 1) < (LANES // 2)
        r2 = lax.broadcasted_iota(jnp.int32, (LANES, 2 * LANES), 0)
        c2 = lax.broadcasted_iota(jnp.int32, (LANES, 2 * LANES), 1)
        state_mask = (r2 < LANES // 2) == (c2 < LANES)
        zero_v = jnp.zeros((CHUNK, dv_pair // 2), BF16)
        state = [s_scr[hp] for hp in range(n_pairs)]

        def gla_early(c):
            rows = slice(c * CHUNK, (c + 1) * CHUNK)
            for hp in range(n_pairs):
                lanes = slice(hp * LANES, (hp + 1) * LANES)
                b2 = bp_scr[rows, lanes]
                q2 = qp_scr[rows, lanes]
                k2 = kp_scr[rows, lanes]
                v2 = vp_scr[rows, hp * dv_pair:(hp + 1) * dv_pair]
                a_hp = ad_scr[rows, lanes]
                for s in OFF_LEVELS:
                    refs = [c * CHUNK + ((r // s) | 1) * s - 1 for r in range(0, CHUNK, s)]
                    bref = jnp.concatenate(
                        [jnp.broadcast_to(bp_scr[r:r + 1, lanes], (s, LANES)) for r in refs], axis=0)
                    odd = ((ri // s) % 2) == 1
                    qs = jnp.where(odd, q2 * jnp.exp(jnp.minimum(b2 - bref, 0.0)), 0.0)
                    ks = jnp.where(odd, 0.0, k2 * jnp.exp(jnp.minimum(bref - b2, 0.0)))
                    kbd = jnp.concatenate([jnp.where(lane_lo, ks, 0.0), jnp.where(lane_lo, 0.0, ks)], axis=0)
                    off = _nt_dot(qs.astype(BF16), kbd.astype(BF16))
                    valid = odd & ((ci // s) == (ri // s) - 1)
                    a_hp = a_hp + jnp.where(valid, off, 0.0)
                qe = q2 * jnp.exp(b2)
                lhs = jnp.concatenate([qe, a_hp], axis=1).astype(BF16)
                b_last = bp_scr[(c + 1) * CHUNK - 1:(c + 1) * CHUNK, lanes]
                kdec = k2 * jnp.exp(jnp.minimum(b_last - b2, 0.0))
                kdec_t = jnp.transpose(jnp.concatenate([kdec, jnp.zeros_like(kdec)], axis=0))
                upd = jnp.dot(kdec_t[:, 0:CHUNK].astype(BF16), v2, preferred_element_type=F32)
                dcol = jnp.transpose(jnp.broadcast_to(jnp.exp(b_last), (LANES, LANES)))
                lhs_scr[c, hp] = lhs
                dcol_scr[c, hp] = dcol
                upd_scr[c, hp] = jnp.where(state_mask, upd, 0.0)

        def gla_late(c):
            rows = slice(c * CHUNK, (c + 1) * CHUNK)
            for hp in range(n_pairs):
                v2 = vp_scr[rows, hp * dv_pair:(hp + 1) * dv_pair]
                vbd = jnp.concatenate(
                    [jnp.concatenate([v2[:, 0:dv_pair // 2], zero_v], axis=1),
                     jnp.concatenate([zero_v, v2[:, dv_pair // 2:dv_pair]], axis=1)], axis=0)
                rhs = jnp.concatenate([state[hp].astype(BF16), vbd], axis=0)
                o_scr[rows, hp * dv_pair:(hp + 1) * dv_pair] = jnp.dot(
                    lhs_scr[c, hp], rhs, preferred_element_type=F32)
                dcol = dcol_scr[c, hp]
                state[hp] = state[hp] * jnp.concatenate([dcol, dcol], axis=1) + upd_scr[c, hp]

        mod = mod_ref[0]
        hb = _rms_mod(x_ref[0], mod[:, 0:d_model], mod[:, d_model:2 * d_model]).astype(BF16)

        def proj(lo, hi):
            return jnp.dot(hb, w_in_ref[:, lo:hi], preferred_element_type=F32)

        assert n_chunks == 8 and dk_all == 2 * LANES and d_pool == 4 * LANES and dv_all == 4 * LANES
        gla_early(0)
        z = jnp.dot(hb, w_z_ref[...], preferred_element_type=F32)
        gla_early(1)
        q_scr[...] = proj(o_q, o_k) * (float(dk_all // GLA_HEADS) ** -0.5)
        gla_late(0)
        gpre = jnp.dot(z.astype(BF16), w_g2_ref[...], preferred_element_type=F32) + b_gate_ref[...]
        glog = (jnp.minimum(gpre, 0.0) - jnp.log1p(jnp.exp(-jnp.abs(gpre)))) * (1.0 / GATE_NORM)
        g_hi = glog.astype(BF16)
        g_lo = (glog - g_hi.astype(F32)).astype(BF16)
        gla_early(2)
        k_scr[...] = proj(o_k, o_v)
        gla_late(1)
        gla_early(3)
        uh_scr[0:POOL_HIST, :] = histp_scr[...]
        uh_scr[POOL_HIST:POOL_HIST + tile, 0:2 * LANES] = proj(0, 2 * LANES)
        gla_late(2)
        gla_early(4)
        uh_scr[POOL_HIST:POOL_HIST + tile, 2 * LANES:d_pool] = proj(2 * LANES, d_pool)
        gla_late(3)
        cs = jnp.dot(tri_ref[...], jnp.concatenate([g_hi, g_lo], axis=1), preferred_element_type=F32)
        b_scr[...] = cs[:, 0:dk_all] + cs[:, dk_all:2 * dk_all]
        gla_early(5)
        v_scr[:, 0:2 * LANES] = proj(o_v, o_v + 2 * LANES).astype(BF16)
        gla_late(4)
        gla_early(6)
        v_scr[:, 2 * LANES:dv_all] = proj(o_v + 2 * LANES, o_r).astype(BF16)
        gla_late(5)

        pos = (t_cur * tile + 1 + lax.broadcasted_iota(jnp.int32, (tile, 1), 0)).astype(F32)
        for gi, w in enumerate(POOL_WINDOWS):
            cols = slice(gi * LANES, (gi + 1) * LANES)
            acc = uh_scr[POOL_HIST:POOL_HIST + tile, cols]
            u_g = acc
            for s in range(1, w):
                acc = acc + uh_scr[POOL_HIST - s:POOL_HIST - s + tile, cols]
            cnt = jnp.minimum(pos, float(w))
            d_g = (acc / cnt - u_g).astype(BF16)
            y_g = jnp.dot(d_g, w_pool_ref[gi], preferred_element_type=F32)
            yp_scr[:, cols] = (y_g * pool_scale_ref[:, cols]).astype(BF16)
        hist_scr[...] = uh_scr[tile:tile + POOL_HIST, :]

        gla_early(7)
        r_scr[:, 0:2 * LANES] = proj(o_r, o_r + 2 * LANES)
        gla_late(6)
        gla_late(7)
        for hp in range(n_pairs):
            s_scr[hp] = state[hp]

        o = o_scr[...]
        ms = jnp.dot((o * o).astype(BF16), hmean_ref[...], preferred_element_type=F32)
        r_scr[:, 2 * LANES:dv_all] = proj(o_r + 2 * LANES, o_r + dv_all)
        y_gla = o * lax.rsqrt(ms + EPS) * gla_norm_ref[...] * _silu(rp_scr[...])
        y = jnp.concatenate([ypp_scr[...], y_gla.astype(BF16)], axis=1)
        g1 = modp_scr[:, 2 * d_model:3 * d_model]
        o_ref[0] = xp_scr[...] + g1 * jnp.dot(y, w_out_ref[...], preferred_element_type=F32)

        for c in range(n_chunks):
            for hp in range(n_pairs):
                lanes = slice(hp * LANES, (hp + 1) * LANES)
                for blk2 in range(CHUNK // (2 * SUB)):
                    rows = []
                    for half in range(2):
                        r0 = c * CHUNK + (2 * blk2 + half) * SUB
                        b8 = b_scr[r0:r0 + SUB, lanes]
                        q8 = q_scr[r0:r0 + SUB, lanes]
                        pj = []
                        for j in range(SUB):
                            bj = b_scr[r0 + j:r0 + j + 1, lanes]
                            kj = k_scr[r0 + j:r0 + j + 1, lanes]
                            e = jnp.exp(jnp.minimum(b8 - bj, 0.0))
                            pj.append(q8 * e * kj)
                        rows.append(jnp.concatenate(pj, axis=1))
                    blk = jnp.concatenate(rows, axis=0).astype(BF16)
                    r0 = c * CHUNK + blk2 * 2 * SUB
                    pc_scr[hp, r0:r0 + 2 * SUB, :] = blk

        xc_scr[...] = x_ref[0]
        modc_scr[...] = mod

    @pl.when(i == 0)
    def _():
        for ref in set1:
            ref[...] = jnp.zeros(ref.shape, ref.dtype)

    @pl.when(t_cur == 0)
    def _():
        hist0_scr[...] = jnp.zeros(hist0_scr.shape, F32)
        hist1_scr[...] = jnp.zeros(hist1_scr.shape, F32)

    @pl.when(t_prev == 0)
    def _():
        s_scr[...] = jnp.zeros(s_scr.shape, F32)

    @pl.when(i % 2 == 0)
    def _():
        step(set0, set1)

    @pl.when(i % 2 == 1)
    def _():
        step(set1, set0)


def _mixer(x, mods, layer, w_in, w_z, w_g2, b_gate, w_pool, pool_scale, gla_norm, w_out, tri, wred, hmean):
    bsz, seq, d_model = x.shape
    d_pool = pool_scale.shape[-1]
    dk_all = b_gate.shape[-1]
    dv_all = gla_norm.shape[-1]
    tile = min(TILE_MIX, seq)
    n_t = seq // tile
    n_pairs = dk_all // LANES
    kern = functools.partial(_mixer_kernel, tile=tile, n_t=n_t, n_tiles=bsz * n_t, d_model=d_model,
                             d_pool=d_pool, dk_all=dk_all, dv_all=dv_all)
    params = (w_in, w_z, w_g2, b_gate, w_pool, pool_scale, gla_norm, w_out)
    consts = (tri, wred, hmean)
    x_cur, x_prev, mod_cur = _tile_specs(bsz, n_t, tile, d_model, mods.shape[-1], layer)

    def stage():
        return [
            pltpu.VMEM((tile, dk_all), F32),
            pltpu.VMEM((tile, dk_all), F32),
            pltpu.VMEM((tile, dk_all), F32),
            pltpu.VMEM((tile, dv_all), BF16),
            pltpu.VMEM((n_pairs, tile, SUB * LANES), BF16),
            pltpu.VMEM((tile, d_pool), BF16),
            pltpu.VMEM((tile, dv_all), F32),
            pltpu.VMEM((tile, d_model), F32),
            pltpu.VMEM((1, mods.shape[-1]), F32),
            pltpu.VMEM((POOL_HIST, d_pool), F32),
        ]

    return pl.pallas_call(
        kern,
        grid=(bsz * n_t + 1,),
        in_specs=[x_cur, mod_cur] + [_layer_spec(a.shape, layer) for a in params]
                 + [_const_spec(a.shape) for a in consts],
        out_specs=x_prev,
        out_shape=jax.ShapeDtypeStruct(x.shape, F32),
        scratch_shapes=[
            pltpu.VMEM((POOL_HIST + tile, d_pool), F32),
            pltpu.VMEM((n_pairs, LANES, dv_all // n_pairs), F32),
            pltpu.VMEM((tile, dk_all), F32),
            pltpu.VMEM((tile, dv_all), F32),
            pltpu.VMEM((tile // CHUNK, n_pairs, CHUNK, 2 * LANES), BF16),
            pltpu.VMEM((tile // CHUNK, n_pairs, LANES, LANES), F32),
            pltpu.VMEM((tile // CHUNK, n_pairs, LANES, 2 * LANES), F32),
        ] + stage() + stage(),
        compiler_params=pltpu.CompilerParams(dimension_semantics=("arbitrary",),
                                             vmem_limit_bytes=VMEM_LIMIT),
        name="token_mixer",
    )(x, mods, *params, *consts)


def _ffn_kernel(x_ref, mod_ref, w_up_ref, cw_ref, cb_ref, w_dn_ref, fn_ref, o_ref,
                carry_scr, act_scr, *, tile, n_t, d_model, d_ff, final):
    i = pl.program_id(0)

    @pl.when(i % n_t == 0)
    def _():
        carry_scr[...] = jnp.zeros(carry_scr.shape, F32)

    x = x_ref[0]
    mod = mod_ref[0]
    sh2 = mod[:, 3 * d_model:4 * d_model]
    sc2 = mod[:, 4 * d_model:5 * d_model]
    g2 = mod[:, 5 * d_model:6 * d_model]
    hb = _rms_mod(x, sh2, sc2).astype(BF16)

    row8 = lax.broadcasted_iota(jnp.int32, (8, FF_COLS), 0)
    for c in range(d_ff // FF_COLS):
        halves = []
        for half in range(2):
            cols = slice(half * d_ff + c * FF_COLS, half * d_ff + (c + 1) * FF_COLS)
            up = jnp.dot(hb, w_up_ref[:, cols], preferred_element_type=F32)
            prev = carry_scr[:, cols]
            carry_scr[:, cols] = up[tile - 8:tile, :]
            r1 = pltpu.roll(up, 1, axis=0)
            r2 = pltpu.roll(up, 2, axis=0)
            h1 = jnp.where(row8 < 1, pltpu.roll(prev, 1, axis=0), r1[0:8, :])
            h2 = jnp.where(row8 < 2, pltpu.roll(prev, 2, axis=0), r2[0:8, :])
            up1 = jnp.concatenate([h1, r1[8:, :]], axis=0)
            up2 = jnp.concatenate([h2, r2[8:, :]], axis=0)
            halves.append(cw_ref[2:3, cols] * up + cw_ref[1:2, cols] * up1
                          + cw_ref[0:1, cols] * up2 + cb_ref[:, cols])
        act_scr[:, c * FF_COLS:(c + 1) * FF_COLS] = (_silu(halves[0]) * halves[1]).astype(BF16)
    out = x + g2 * jnp.dot(act_scr[...], w_dn_ref[...], preferred_element_type=F32)
    if final:
        ms = jnp.mean(out * out, axis=-1, keepdims=True)
        out = out * lax.rsqrt(ms + EPS) * fn_ref[...]
    o_ref[0] = out


def _ffn(x, mods, layer, w_up, conv_w, conv_b, w_down, final_norm, *, final):
    bsz, seq, d_model = x.shape
    d_ff = w_down.shape[1]
    tile = min(TILE_FFN, seq)
    n_t = seq // tile
    kern = functools.partial(_ffn_kernel, tile=tile, n_t=n_t, d_model=d_model, d_ff=d_ff, final=final)
    params = (w_up, conv_w, conv_b, w_down)
    x_spec = pl.BlockSpec((1, tile, d_model), lambda i: (i // n_t, i % n_t, 0))
    mod_spec = pl.BlockSpec((None, 1, 1, mods.shape[-1]), lambda i: (layer, i // n_t, 0, 0))
    return pl.pallas_call(
        kern,
        grid=(bsz * n_t,),
        in_specs=[x_spec, mod_spec] + [_layer_spec(a.shape, layer) for a in params]
                 + [_const_spec(final_norm.shape)],
        out_specs=x_spec,
        out_shape=jax.ShapeDtypeStruct(x.shape, F32),
        scratch_shapes=[
            pltpu.VMEM((8, 2 * d_ff), F32),
            pltpu.VMEM((tile, d_ff), BF16),
        ],
        compiler_params=pltpu.CompilerParams(dimension_semantics=("arbitrary",),
                                             vmem_limit_bytes=VMEM_LIMIT),
        name="conv_ffn",
    )(x, mods, *params, final_norm)


def _mixer_constants(tile, dk_all, dv_all):
    r = np.arange(tile)
    tri = ((r[:, None] // CHUNK == r[None, :] // CHUNK) & (r[:, None] >= r[None, :]))
    kk = np.arange(SUB * LANES)
    wred = (kk[:, None] // LANES + (LANES // 2) * ((kk[:, None] % LANES) // (LANES // 2))
            == np.arange(LANES)[None, :])
    dv = dv_all // GLA_HEADS
    e = np.arange(dv_all)
    hmean = (e[:, None] // dv == e[None, :] // dv) / float(dv)
    return (jnp.asarray(tri, BF16), jnp.asarray(wred, BF16), jnp.asarray(hmean, BF16))


def kernel(x, c, ada_w, ada_b, w_in, w_gate2, b_gate, w_pool, pool_scale, gla_norm, w_out, w_up,
           conv_w, conv_b, w_down, final_norm):
    depth = ada_w.shape[0]
    seq = x.shape[1]
    d_pool = pool_scale.shape[-1]
    dk_all = b_gate.shape[-1]
    dv_all = gla_norm.shape[-1]
    n_main = d_pool + 2 * dk_all + 2 * dv_all
    assert w_in.shape[-1] == n_main + GATE_RANK and dk_all == 2 * LANES and seq % CHUNK == 0

    mods = _modulation(c, ada_w, ada_b)[:, :, None, :]
    tri, wred, hmean = _mixer_constants(min(TILE_MIX, seq), dk_all, dv_all)
    pad_z = LANES - GATE_RANK
    w_in_b = w_in.astype(BF16)
    w_z = jnp.pad(w_in[:, :, n_main:], ((0, 0), (0, 0), (0, pad_z))).astype(BF16)
    w_g2 = jnp.pad(w_gate2, ((0, 0), (0, pad_z), (0, 0))).astype(BF16)
    w_pool_b, w_out_b, w_up_b, w_down_b = (w.astype(BF16) for w in (w_pool, w_out, w_up, w_down))

    def rows(a):
        return a[:, None, :]

    for l in range(depth):
        x = _mixer(x, mods, l, w_in_b, w_z, w_g2, rows(b_gate), w_pool_b, rows(pool_scale),
                   rows(gla_norm), w_out_b, tri, wred, hmean)
        x = _ffn(x, mods, l, w_up_b, conv_w, rows(conv_b), w_down_b, final_norm[None, :],
                 final=(l == depth - 1))
    return x
```

```python
import functools

import jax
import jax.numpy as jnp
import numpy as np
from jax import lax
from jax.experimental import pallas as pl
from jax.experimental.pallas import tpu as pltpu

F32 = jnp.float32
BF16 = jnp.bfloat16

EPS = 1e-6
POOL_WINDOWS = (2, 4, 8, 16)
POOL_HIST = 16
GLA_HEADS = 4
GATE_RANK = 16
GATE_NORM = 16.0
CHUNK = 64
SUB = 8
OFF_LEVELS = (32, 16, 8)
CONV_W = 3
LANES = 128
VMEM_LIMIT = 56 * 1024 * 1024

TILE_MIX = 512
TILE_FFN = 1024
FF_COLS = 256
MOD_COLS = 1024


def _silu(x):
    h = 0.5 * x
    return h + h * jnp.tanh(h)


def _rms_mod(x, shift, scale):
    ms = jnp.mean(x * x, axis=-1, keepdims=True)
    return x * lax.rsqrt(ms + EPS) * (1.0 + scale) + shift


def _nt_dot(a, b):
    return lax.dot_general(a, b, (((1,), (1,)), ((), ())), preferred_element_type=F32)


def _mod_kernel(c_ref, w_ref, b_ref, o_ref):
    ca = _silu(c_ref[...])
    c_hi = ca.astype(BF16)
    c_lo = (ca - c_hi.astype(F32)).astype(BF16)
    w = w_ref[0]
    w_hi = w.astype(BF16)
    w_lo = (w - w_hi.astype(F32)).astype(BF16)
    bsz = ca.shape[0]
    main = jnp.dot(jnp.concatenate([c_hi, c_lo], axis=0), w_hi, preferred_element_type=F32)
    o_ref[0] = (main[0:bsz] + main[bsz:2 * bsz]
                + jnp.dot(c_hi, w_lo, preferred_element_type=F32) + b_ref[0])


def _modulation(c, ada_w, ada_b):
    n_layers, d, n = ada_w.shape
    bsz = c.shape[0]
    return pl.pallas_call(
        _mod_kernel,
        grid=(n_layers, n // MOD_COLS),
        in_specs=[
            pl.BlockSpec((bsz, d), lambda l, j: (0, 0)),
            pl.BlockSpec((1, d, MOD_COLS), lambda l, j: (l, 0, j)),
            pl.BlockSpec((1, 1, MOD_COLS), lambda l, j: (l, 0, j)),
        ],
        out_specs=pl.BlockSpec((1, bsz, MOD_COLS), lambda l, j: (l, 0, j)),
        out_shape=jax.ShapeDtypeStruct((n_layers, bsz, n), F32),
        compiler_params=pltpu.CompilerParams(dimension_semantics=("arbitrary", "arbitrary")),
        name="adaln_mod",
    )(c, ada_w, ada_b.reshape(n_layers, 1, n))


def _tile_specs(bsz, n_t, tile, d_model, mod_width, layer):
    last = bsz * n_t - 1

    def cur(i):
        return jnp.minimum(i, last)

    def prev(i):
        return jnp.maximum(i - 1, 0)

    x_cur = pl.BlockSpec((1, tile, d_model), lambda i: (cur(i) // n_t, cur(i) % n_t, 0))
    x_prev = pl.BlockSpec((1, tile, d_model), lambda i: (prev(i) // n_t, prev(i) % n_t, 0))
    mod_cur = pl.BlockSpec((None, 1, 1, mod_width), lambda i: (layer, cur(i) // n_t, 0, 0))
    return x_cur, x_prev, mod_cur


def _const_spec(shape):
    zeros = (0,) * len(shape)
    return pl.BlockSpec(shape, lambda i: zeros, pipeline_mode=pl.Buffered(1))


def _layer_spec(shape, layer):
    idx = (layer,) + (0,) * (len(shape) - 1)
    return pl.BlockSpec((None,) + tuple(shape[1:]), lambda i: idx, pipeline_mode=pl.Buffered(1))


def _mixer_kernel(x_ref, mod_ref, w_in_ref, w_z_ref, w_g2_ref, b_gate_ref,
                  w_pool_ref, pool_scale_ref, gla_norm_ref, w_out_ref, tri_ref, wred_ref, hmean_ref,
                  o_ref,
                  uh_scr, s_scr, ad_scr, o_scr, lhs_scr, dcol_scr, upd_scr,
                  b0_scr, q0_scr, k0_scr, v0_scr, pc0_scr, yp0_scr, r0_scr, x0_scr, mod0_scr, hist0_scr,
                  b1_scr, q1_scr, k1_scr, v1_scr, pc1_scr, yp1_scr, r1_scr, x1_scr, mod1_scr, hist1_scr,
                  *, tile, n_t, n_tiles, d_model, d_pool, dk_all, dv_all):
    i = pl.program_id(0)
    t_cur = jnp.minimum(i, n_tiles - 1) % n_t
    t_prev = jnp.maximum(i - 1, 0) % n_t
    n_chunks = tile // CHUNK
    n_pairs = dk_all // LANES
    dv_pair = dv_all // n_pairs
    set0 = (b0_scr, q0_scr, k0_scr, v0_scr, pc0_scr, yp0_scr, r0_scr, x0_scr, mod0_scr, hist0_scr)
    set1 = (b1_scr, q1_scr, k1_scr, v1_scr, pc1_scr, yp1_scr, r1_scr, x1_scr, mod1_scr, hist1_scr)

    def step(cur, prv):
        b_scr, q_scr, k_scr, v_scr, pc_scr, yp_scr, r_scr, xc_scr, modc_scr, hist_scr = cur
        bp_scr, qp_scr, kp_scr, vp_scr, pcp_scr, ypp_scr, rp_scr, xp_scr, modp_scr, histp_scr = prv
        o_q = d_pool
        o_k = o_q + dk_all
        o_v = o_k + dk_all
        o_r = o_v + dv_all

        row_l = lax.broadcasted_iota(jnp.int32, (SUB, LANES), 0)
        col_l = lax.broadcasted_iota(jnp.int32, (SUB, LANES), 1) % (LANES // 2)
        tril8 = row_l >= col_l
        for hp in range(n_pairs):
            ad = jnp.dot(pcp_scr[hp], wred_ref[...], preferred_element_type=F32)
            placed = []
            for rb in range(tile // SUB):
                a8 = jnp.where(tril8, ad[rb * SUB:(rb + 1) * SUB, :], 0.0)
                shift = (rb % (CHUNK // SUB)) * SUB
                placed.append(pltpu.roll(a8, shift, axis=1) if shift else a8)
            ad_scr[:, hp * LANES:(hp + 1) * LANES] = jnp.concatenate(placed, axis=0)

        ri = lax.broadcasted_iota(jnp.int32, (CHUNK, LANES), 0)
        ci = lax.broadcasted_iota(jnp.int32, (CHUNK, LANES), 1) % (LANES // 2)
        lane_lo = lax.broadcasted_iota(jnp.int32, (CHUNK, LANES), 1) < (LANES // 2)
        r2 = lax.broadcasted_iota(jnp.int32, (LANES, 2 * LANES), 0)
        c2 = lax.broadcasted_iota(jnp.int32, (LANES, 2 * LANES), 1)
        state_mask = (r2 < LANES // 2) == (c2 < LANES)
        zero_v = jnp.zeros((CHUNK, dv_pair // 2), BF16)
        state = [s_scr[hp] for hp in range(n_pairs)]

        def gla_early(c):
            rows = slice(c * CHUNK, (c + 1) * CHUNK)
            for hp in range(n_pairs):
                lanes = slice(hp * LANES, (hp + 1) * LANES)
                b2 = bp_scr[rows, lanes]
                q2 = qp_scr[rows, lanes]
                k2 = kp_scr[rows, lanes]
                v2 = vp_scr[rows, hp * dv_pair:(hp + 1) * dv_pair]
                a_hp = ad_scr[rows, lanes]
                for s in OFF_LEVELS:
                    refs = [c * CHUNK + ((r // s) | 1) * s - 1 for r in range(0, CHUNK, s)]
                    bref = jnp.concatenate(
                        [jnp.broadcast_to(bp_scr[r:r + 1, lanes], (s, LANES)) for r in refs], axis=0)
                    odd = ((ri // s) % 2) == 1
                    qs = jnp.where(odd, q2 * jnp.exp(jnp.minimum(b2 - bref, 0.0)), 0.0)
                    ks = jnp.where(odd, 0.0, k2 * jnp.exp(jnp.minimum(bref - b2, 0.0)))
                    kbd = jnp.concatenate([jnp.where(lane_lo, ks, 0.0), jnp.where(lane_lo, 0.0, ks)], axis=0)
                    off = _nt_dot(qs.astype(BF16), kbd.astype(BF16))
                    valid = odd & ((ci // s) == (ri // s) - 1)
                    a_hp = a_hp + jnp.where(valid, off, 0.0)
                qe = q2 * jnp.exp(b2)
                lhs = jnp.concatenate([qe, a_hp], axis=1).astype(BF16)
                b_last = bp_scr[(c + 1) * CHUNK - 1:(c + 1) * CHUNK, lanes]
                kdec = k2 * jnp.exp(jnp.minimum(b_last - b2, 0.0))
                kdec_t = jnp.transpose(jnp.concatenate([kdec, jnp.zeros_like(kdec)], axis=0))
                upd = jnp.dot(kdec_t[:, 0:CHUNK].astype(BF16), v2, preferred_element_type=F32)
                dcol = jnp.transpose(jnp.broadcast_to(jnp.exp(b_last), (LANES, LANES)))
                lhs_scr[c, hp] = lhs
                dcol_scr[c, hp] = dcol
                upd_scr[c, hp] = jnp.where(state_mask, upd, 0.0)

        def gla_late(c):
            rows = slice(c * CHUNK, (c + 1) * CHUNK)
            for hp in range(n_pairs):
                v2 = vp_scr[rows, hp * dv_pair:(hp + 1) * dv_pair]
                vbd = jnp.concatenate(
                    [jnp.concatenate([v2[:, 0:dv_pair // 2], zero_v], axis=1),
                     jnp.concatenate([zero_v, v2[:, dv_pair // 2:dv_pair]], axis=1)], axis=0)
                rhs = jnp.concatenate([state[hp].astype(BF16), vbd], axis=0)
                o_scr[rows, hp * dv_pair:(hp + 1) * dv_pair] = jnp.dot(
                    lhs_scr[c, hp], rhs, preferred_element_type=F32)
                dcol = dcol_scr[c, hp]
                state[hp] = state[hp] * jnp.concatenate([dcol, dcol], axis=1) + upd_scr[c, hp]

        mod = mod_ref[0]
        hb = _rms_mod(x_ref[0], mod[:, 0:d_model], mod[:, d_model:2 * d_model]).astype(BF16)

        def proj(lo, hi):
            return jnp.dot(hb, w_in_ref[:, lo:hi], preferred_element_type=F32)

        assert n_chunks == 8 and dk_all == 2 * LANES and d_pool == 4 * LANES and dv_all == 4 * LANES
        gla_early(0)
        z = jnp.dot(hb, w_z_ref[...], preferred_element_type=F32)
        gla_early(1)
        q_scr[...] = proj(o_q, o_k) * (float(dk_all // GLA_HEADS) ** -0.5)
        gla_late(0)
        gpre = jnp.dot(z.astype(BF16), w_g2_ref[...], preferred_element_type=F32) + b_gate_ref[...]
        glog = (jnp.minimum(gpre, 0.0) - jnp.log1p(jnp.exp(-jnp.abs(gpre)))) * (1.0 / GATE_NORM)
        g_hi = glog.astype(BF16)
        g_lo = (glog - g_hi.astype(F32)).astype(BF16)
        gla_early(2)
        k_scr[...] = proj(o_k, o_v)
        gla_late(1)
        gla_early(3)
        uh_scr[0:POOL_HIST, :] = histp_scr[...]
        uh_scr[POOL_HIST:POOL_HIST + tile, 0:2 * LANES] = proj(0, 2 * LANES)
        gla_late(2)
        gla_early(4)
        uh_scr[POOL_HIST:POOL_HIST + tile, 2 * LANES:d_pool] = proj(2 * LANES, d_pool)
        gla_late(3)
        cs = jnp.dot(tri_ref[...], jnp.concatenate([g_hi, g_lo], axis=1), preferred_element_type=F32)
        b_scr[...] = cs[:, 0:dk_all] + cs[:, dk_all:2 * dk_all]
        gla_early(5)
        v_scr[:, 0:2 * LANES] = proj(o_v, o_v + 2 * LANES).astype(BF16)
        gla_late(4)
        gla_early(6)
        v_scr[:, 2 * LANES:dv_all] = proj(o_v + 2 * LANES, o_r).astype(BF16)
        gla_late(5)

        pos = (t_cur * tile + 1 + lax.broadcasted_iota(jnp.int32, (tile, 1), 0)).astype(F32)
        for gi, w in enumerate(POOL_WINDOWS):
            cols = slice(gi * LANES, (gi + 1) * LANES)
            acc = uh_scr[POOL_HIST:POOL_HIST + tile, cols]
            u_g = acc
            for s in range(1, w):
                acc = acc + uh_scr[POOL_HIST - s:POOL_HIST - s + tile, cols]
            cnt = jnp.minimum(pos, float(w))
            d_g = (acc / cnt - u_g).astype(BF16)
            y_g = jnp.dot(d_g, w_pool_ref[gi], preferred_element_type=F32)
            yp_scr[:, cols] = (y_g * pool_scale_ref[:, cols]).astype(BF16)
        hist_scr[...] = uh_scr[tile:tile + POOL_HIST, :]

        gla_early(7)
        r_scr[:, 0:2 * LANES] = proj(o_r, o_r + 2 * LANES)
        gla_late(6)
        gla_late(7)
        for hp in range(n_pairs):
            s_scr[hp] = state[hp]

        o = o_scr[...]
        ms = jnp.dot((o * o).astype(BF16), hmean_ref[...], preferred_element_type=F32)
        r_scr[:, 2 * LANES:dv_all] = proj(o_r + 2 * LANES, o_r + dv_all)
        y_gla = o * lax.rsqrt(ms + EPS) * gla_norm_ref[...] * _silu(rp_scr[...])
        y = jnp.concatenate([ypp_scr[...], y_gla.astype(BF16)], axis=1)
        g1 = modp_scr[:, 2 * d_model:3 * d_model]
        o_ref[0] = xp_scr[...] + g1 * jnp.dot(y, w_out_ref[...], preferred_element_type=F32)

        for c in range(n_chunks):
            for hp in range(n_pairs):
                lanes = slice(hp * LANES, (hp + 1) * LANES)
                for blk2 in range(CHUNK // (2 * SUB)):
                    rows = []
                    for half in range(2):
                        r0 = c * CHUNK + (2 * blk2 + half) * SUB
                        b8 = b_scr[r0:r0 + SUB, lanes]
                        q8 = q_scr[r0:r0 + SUB, lanes]
                        pj = []
                        for j in range(SUB):
                            bj = b_scr[r0 + j:r0 + j + 1, lanes]
                            kj = k_scr[r0 + j:r0 + j + 1, lanes]
                            e = jnp.exp(jnp.minimum(b8 - bj, 0.0))
                            pj.append(q8 * e * kj)
                        rows.append(jnp.concatenate(pj, axis=1))
                    blk = jnp.concatenate(rows, axis=0).astype(BF16)
                    r0 = c * CHUNK + blk2 * 2 * SUB
                    pc_scr[hp, r0:r0 + 2 * SUB, :] = blk

        xc_scr[...] = x_ref[0]
        modc_scr[...] = mod

    @pl.when(i == 0)
    def _():
        for ref in set1:
            ref[...] = jnp.zeros(ref.shape, ref.dtype)

    @pl.when(t_cur == 0)
    def _():
        hist0_scr[...] = jnp.zeros(hist0_scr.shape, F32)
        hist1_scr[...] = jnp.zeros(hist1_scr.shape, F32)

    @pl.when(t_prev == 0)
    def _():
        s_scr[...] = jnp.zeros(s_scr.shape, F32)

    @pl.when(i % 2 == 0)
    def _():
        step(set0, set1)

    @pl.when(i % 2 == 1)
    def _():
        step(set1, set0)


def _mixer(x, mods, layer, w_in, w_z, w_g2, b_gate, w_pool, pool_scale, gla_norm, w_out, tri, wred, hmean):
    bsz, seq, d_model = x.shape
    d_pool = pool_scale.shape[-1]
    dk_all = b_gate.shape[-1]
    dv_all = gla_norm.shape[-1]
    tile = min(TILE_MIX, seq)
    n_t = seq // tile
    n_pairs = dk_all // LANES
    kern = functools.partial(_mixer_kernel, tile=tile, n_t=n_t, n_tiles=bsz * n_t, d_model=d_model,
                             d_pool=d_pool, dk_all=dk_all, dv_all=dv_all)
    params = (w_in, w_z, w_g2, b_gate, w_pool, pool_scale, gla_norm, w_out)
    consts = (tri, wred, hmean)
    x_cur, x_prev, mod_cur = _tile_specs(bsz, n_t, tile, d_model, mods.shape[-1], layer)

    def stage():
        return [
            pltpu.VMEM((tile, dk_all), F32),
            pltpu.VMEM((tile, dk_all), F32),
            pltpu.VMEM((tile, dk_all), F32),
            pltpu.VMEM((tile, dv_all), BF16),
            pltpu.VMEM((n_pairs, tile, SUB * LANES), BF16),
            pltpu.VMEM((tile, d_pool), BF16),
            pltpu.VMEM((tile, dv_all), F32),
            pltpu.VMEM((tile, d_model), F32),
            pltpu.VMEM((1, mods.shape[-1]), F32),
            pltpu.VMEM((POOL_HIST, d_pool), F32),
        ]

    return pl.pallas_call(
        kern,
        grid=(bsz * n_t + 1,),
        in_specs=[x_cur, mod_cur] + [_layer_spec(a.shape, layer) for a in params]
                 + [_const_spec(a.shape) for a in consts],
        out_specs=x_prev,
        out_shape=jax.ShapeDtypeStruct(x.shape, F32),
        scratch_shapes=[
            pltpu.VMEM((POOL_HIST + tile, d_pool), F32),
            pltpu.VMEM((n_pairs, LANES, dv_all // n_pairs), F32),
            pltpu.VMEM((tile, dk_all), F32),
            pltpu.VMEM((tile, dv_all), F32),
            pltpu.VMEM((tile // CHUNK, n_pairs, CHUNK, 2 * LANES), BF16),
            pltpu.VMEM((tile // CHUNK, n_pairs, LANES, LANES), F32),
            pltpu.VMEM((tile // CHUNK, n_pairs, LANES, 2 * LANES), F32),
        ] + stage() + stage(),
        compiler_params=pltpu.CompilerParams(dimension_semantics=("arbitrary",),
                                             vmem_limit_bytes=VMEM_LIMIT),
        name="token_mixer",
    )(x, mods, *params, *consts)


def _ffn_kernel(x_ref, mod_ref, w_up_ref, cw_ref, cb_ref, w_dn_ref, fn_ref, o_ref,
                carry_scr, act_scr, *, tile, n_t, d_model, d_ff, final):
    i = pl.program_id(0)

    @pl.when(i % n_t == 0)
    def _():
        carry_scr[...] = jnp.zeros(carry_scr.shape, F32)

    x = x_ref[0]
    mod = mod_ref[0]
    sh2 = mod[:, 3 * d_model:4 * d_model]
    sc2 = mod[:, 4 * d_model:5 * d_model]
    g2 = mod[:, 5 * d_model:6 * d_model]
    hb = _rms_mod(x, sh2, sc2).astype(BF16)

    row8 = lax.broadcasted_iota(jnp.int32, (8, FF_COLS), 0)
    for c in range(d_ff // FF_COLS):
        halves = []
        for half in range(2):
            cols = slice(half * d_ff + c * FF_COLS, half * d_ff + (c + 1) * FF_COLS)
            up = jnp.dot(hb, w_up_ref[:, cols], preferred_element_type=F32)
            prev = carry_scr[:, cols]
            carry_scr[:, cols] = up[tile - 8:tile, :]
            r1 = pltpu.roll(up, 1, axis=0)
            r2 = pltpu.roll(up, 2, axis=0)
            h1 = jnp.where(row8 < 1, pltpu.roll(prev, 1, axis=0), r1[0:8, :])
            h2 = jnp.where(row8 < 2, pltpu.roll(prev, 2, axis=0), r2[0:8, :])
            up1 = jnp.concatenate([h1, r1[8:, :]], axis=0)
            up2 = jnp.concatenate([h2, r2[8:, :]], axis=0)
            halves.append(cw_ref[2:3, cols] * up + cw_ref[1:2, cols] * up1
                          + cw_ref[0:1, cols] * up2 + cb_ref[:, cols])
        act_scr[:, c * FF_COLS:(c + 1) * FF_COLS] = (_silu(halves[0]) * halves[1]).astype(BF16)
    out = x + g2 * jnp.dot(act_scr[...], w_dn_ref[...], preferred_element_type=F32)
    if final:
        ms = jnp.mean(out * out, axis=-1, keepdims=True)
        out = out * lax.rsqrt(ms + EPS) * fn_ref[...]
    o_ref[0] = out


def _ffn(x, mods, layer, w_up, conv_w, conv_b, w_down, final_norm, *, final):
    bsz, seq, d_model = x.shape
    d_ff = w_down.shape[1]
    tile = min(TILE_FFN, seq)
    n_t = seq // tile
    kern = functools.partial(_ffn_kernel, tile=tile, n_t=n_t, d_model=d_model, d_ff=d_ff, final=final)
    params = (w_up, conv_w, conv_b, w_down)
    x_spec = pl.BlockSpec((1, tile, d_model), lambda i: (i // n_t, i % n_t, 0))
    mod_spec = pl.BlockSpec((None, 1, 1, mods.shape[-1]), lambda i: (layer, i // n_t, 0, 0))
    return pl.pallas_call(
        kern,
        grid=(bsz * n_t,),
        in_specs=[x_spec, mod_spec] + [_layer_spec(a.shape, layer) for a in params]
                 + [_const_spec(final_norm.shape)],
        out_specs=x_spec,
        out_shape=jax.ShapeDtypeStruct(x.shape, F32),
        scratch_shapes=[
            pltpu.VMEM((8, 2 * d_ff), F32),
            pltpu.VMEM((tile, d_ff), BF16),
        ],
        compiler_params=pltpu.CompilerParams(dimension_semantics=("arbitrary",),
                                             vmem_limit_bytes=VMEM_LIMIT),
        name="conv_ffn",
    )(x, mods, *params, final_norm)


def _mixer_constants(tile, dk_all, dv_all):
    r = np.arange(tile)
    tri = ((r[:, None] // CHUNK == r[None, :] // CHUNK) & (r[:, None] >= r[None, :]))
    kk = np.arange(SUB * LANES)
    wred = (kk[:, None] // LANES + (LANES // 2) * ((kk[:, None] % LANES) // (LANES // 2))
            == np.arange(LANES)[None, :])
    dv = dv_all // GLA_HEADS
    e = np.arange(dv_all)
    hmean = (e[:, None] // dv == e[None, :] // dv) / float(dv)
    return (jnp.asarray(tri, BF16), jnp.asarray(wred, BF16), jnp.asarray(hmean, BF16))


def kernel(x, c, ada_w, ada_b, w_in, w_gate2, b_gate, w_pool, pool_scale, gla_norm, w_out, w_up,
           conv_w, conv_b, w_down, final_norm):
    depth = ada_w.shape[0]
    seq = x.shape[1]
    d_pool = pool_scale.shape[-1]
    dk_all = b_gate.shape[-1]
    dv_all = gla_norm.shape[-1]
    n_main = d_pool + 2 * dk_all + 2 * dv_all
    assert w_in.shape[-1] == n_main + GATE_RANK and dk_all == 2 * LANES and seq % CHUNK == 0

    mods = _modulation(c, ada_w, ada_b)[:, :, None, :]
    tri, wred, hmean = _mixer_constants(min(TILE_MIX, seq), dk_all, dv_all)
    pad_z = LANES - GATE_RANK
    w_in_b = w_in[:, :, :n_main].astype(BF16)
    w_z = jnp.pad(w_in[:, :, n_main:], ((0, 0), (0, 0), (0, pad_z))).astype(BF16)
    w_g2 = jnp.pad(w_gate2, ((0, 0), (0, pad_z), (0, 0))).astype(BF16)
    w_pool_b, w_out_b, w_up_b, w_down_b = (w.astype(BF16) for w in (w_pool, w_out, w_up, w_down))

    def rows(a):
        return a[:, None, :]

    for l in range(depth):
        x = _mixer(x, mods, l, w_in_b, w_z, w_g2, rows(b_gate), w_pool_b, rows(pool_scale),
                   rows(gla_norm), w_out_b, tri, wred, hmean)
        x = _ffn(x, mods, l, w_up_b, conv_w, rows(conv_b), w_down_b, final_norm[None, :],
                 final=(l == depth - 1))
    return x
```

```python
import functools

import jax
import jax.numpy as jnp
import numpy as np
from jax import lax
from jax.experimental import pallas as pl
from jax.experimental.pallas import tpu as pltpu

F32 = jnp.float32
BF16 = jnp.bfloat16

EPS = 1e-6
POOL_WINDOWS = (2, 4, 8, 16)
POOL_HIST = 16
GLA_HEADS = 4
GATE_RANK = 16
GATE_NORM = 16.0
CHUNK = 64
SUB = 8
OFF_LEVELS = (32, 16, 8)
CONV_W = 3
LANES = 128
VMEM_LIMIT = 56 * 1024 * 1024

TILE_MIX = 512
TILE_FFN = 1024
FF_COLS = 256
MOD_COLS = 1024


def _silu(x):
    h = 0.5 * x
    return h + h * jnp.tanh(h)


def _rms_mod(x, shift, scale):
    ms = jnp.mean(x * x, axis=-1, keepdims=True)
    return x * lax.rsqrt(ms + EPS) * (1.0 + scale) + shift


def _nt_dot(a, b):
    return lax.dot_general(a, b, (((1,), (1,)), ((), ())), preferred_element_type=F32)


def _mod_kernel(c_ref, w_ref, b_ref, o_ref):
    ca = _silu(c_ref[...])
    c_hi = ca.astype(BF16)
    c_lo = (ca - c_hi.astype(F32)).astype(BF16)
    w = w_ref[0]
    w_hi = w.astype(BF16)
    w_lo = (w - w_hi.astype(F32)).astype(BF16)
    bsz = ca.shape[0]
    main = jnp.dot(jnp.concatenate([c_hi, c_lo], axis=0), w_hi, preferred_element_type=F32)
    o_ref[0] = (main[0:bsz] + main[bsz:2 * bsz]
                + jnp.dot(c_hi, w_lo, preferred_element_type=F32) + b_ref[0])


def _modulation(c, ada_w, ada_b):
    n_layers, d, n = ada_w.shape
    bsz = c.shape[0]
    return pl.pallas_call(
        _mod_kernel,
        grid=(n_layers, n // MOD_COLS),
        in_specs=[
            pl.BlockSpec((bsz, d), lambda l, j: (0, 0)),
            pl.BlockSpec((1, d, MOD_COLS), lambda l, j: (l, 0, j)),
            pl.BlockSpec((1, 1, MOD_COLS), lambda l, j: (l, 0, j)),
        ],
        out_specs=pl.BlockSpec((1, bsz, MOD_COLS), lambda l, j: (l, 0, j)),
        out_shape=jax.ShapeDtypeStruct((n_layers, bsz, n), F32),
        compiler_params=pltpu.CompilerParams(dimension_semantics=("arbitrary", "arbitrary")),
        name="adaln_mod",
    )(c, ada_w, ada_b.reshape(n_layers, 1, n))


def _tile_specs(bsz, n_t, tile, d_model, mod_width, layer):
    last = bsz * n_t - 1

    def cur(i):
        return jnp.minimum(i, last)

    def prev(i):
        return jnp.maximum(i - 1, 0)

    x_cur = pl.BlockSpec((1, tile, d_model), lambda i: (cur(i) // n_t, cur(i) % n_t, 0))
    x_prev = pl.BlockSpec((1, tile, d_model), lambda i: (prev(i) // n_t, prev(i) % n_t, 0))
    mod_cur = pl.BlockSpec((None, 1, 1, mod_width), lambda i: (layer, cur(i) // n_t, 0, 0))
    return x_cur, x_prev, mod_cur


def _const_spec(shape):
    zeros = (0,) * len(shape)
    return pl.BlockSpec(shape, lambda i: zeros, pipeline_mode=pl.Buffered(1))


def _layer_spec(shape, layer):
    idx = (layer,) + (0,) * (len(shape) - 1)
    return pl.BlockSpec((None,) + tuple(shape[1:]), lambda i: idx, pipeline_mode=pl.Buffered(1))


def _mixer_kernel(x_ref, mod_ref, w_in_ref, w_z_ref, w_g2_ref, b_gate_ref,
                  w_pool_ref, pool_scale_ref, gla_norm_ref, w_out_ref, tri_ref, wred_ref, hmean_ref,
                  o_ref,
                  uh_scr, s_scr, ad_scr, o_scr, pc_scr, qe_scr, aoff_scr, dcol_scr, upd_scr,
                  b0_scr, q0_scr, k0_scr, v0_scr, yp0_scr, r0_scr, x0_scr, mod0_scr, hist0_scr,
                  b1_scr, q1_scr, k1_scr, v1_scr, yp1_scr, r1_scr, x1_scr, mod1_scr, hist1_scr,
                  *, tile, n_t, n_tiles, d_model, d_pool, dk_all, dv_all):
    i = pl.program_id(0)
    t_cur = jnp.minimum(i, n_tiles - 1) % n_t
    t_prev = jnp.maximum(i - 1, 0) % n_t
    n_chunks = tile // CHUNK
    n_pairs = dk_all // LANES
    dv_pair = dv_all // n_pairs
    set0 = (b0_scr, q0_scr, k0_scr, v0_scr, yp0_scr, r0_scr, x0_scr, mod0_scr, hist0_scr)
    set1 = (b1_scr, q1_scr, k1_scr, v1_scr, yp1_scr, r1_scr, x1_scr, mod1_scr, hist1_scr)

    def step(cur, prv):
        b_scr, q_scr, k_scr, v_scr, yp_scr, r_scr, xc_scr, modc_scr, hist_scr = cur
        bp_scr, qp_scr, kp_scr, vp_scr, ypp_scr, rp_scr, xp_scr, modp_scr, histp_scr = prv
        o_q = d_pool
        o_k = o_q + dk_all
        o_v = o_k + dk_all
        o_r = o_v + dv_all

        for c in range(n_chunks):
            for hp in range(n_pairs):
                lanes = slice(hp * LANES, (hp + 1) * LANES)
                for blk2 in range(CHUNK // (2 * SUB)):
                    rows = []
                    for half in range(2):
                        r0 = c * CHUNK + (2 * blk2 + half) * SUB
                        b8 = bp_scr[r0:r0 + SUB, lanes]
                        q8 = qp_scr[r0:r0 + SUB, lanes]
                        pj = []
                        for j in range(SUB):
                            bj = bp_scr[r0 + j:r0 + j + 1, lanes]
                            kj = kp_scr[r0 + j:r0 + j + 1, lanes]
                            e = jnp.exp(jnp.minimum(b8 - bj, 0.0))
                            pj.append(q8 * e * kj)
                        rows.append(jnp.concatenate(pj, axis=1))
                    blk = jnp.concatenate(rows, axis=0).astype(BF16)
                    r0 = c * CHUNK + blk2 * 2 * SUB
                    pc_scr[hp, r0:r0 + 2 * SUB, :] = blk

        def diag_blocks():
            row_l = lax.broadcasted_iota(jnp.int32, (SUB, LANES), 0)
            col_l = lax.broadcasted_iota(jnp.int32, (SUB, LANES), 1) % (LANES // 2)
            tril8 = row_l >= col_l
            for hp in range(n_pairs):
                ad = jnp.dot(pc_scr[hp], wred_ref[...], preferred_element_type=F32)
                placed = []
                for rb in range(tile // SUB):
                    a8 = jnp.where(tril8, ad[rb * SUB:(rb + 1) * SUB, :], 0.0)
                    shift = (rb % (CHUNK // SUB)) * SUB
                    placed.append(pltpu.roll(a8, shift, axis=1) if shift else a8)
                ad_scr[:, hp * LANES:(hp + 1) * LANES] = jnp.concatenate(placed, axis=0)

        ri = lax.broadcasted_iota(jnp.int32, (CHUNK, LANES), 0)
        ci = lax.broadcasted_iota(jnp.int32, (CHUNK, LANES), 1) % (LANES // 2)
        lane_lo = lax.broadcasted_iota(jnp.int32, (CHUNK, LANES), 1) < (LANES // 2)
        r2 = lax.broadcasted_iota(jnp.int32, (LANES, 2 * LANES), 0)
        c2 = lax.broadcasted_iota(jnp.int32, (LANES, 2 * LANES), 1)
        state_mask = (r2 < LANES // 2) == (c2 < LANES)
        zero_v = jnp.zeros((CHUNK, dv_pair // 2), BF16)
        state = [s_scr[hp] for hp in range(n_pairs)]

        def gla_early(c):
            rows = slice(c * CHUNK, (c + 1) * CHUNK)
            for hp in range(n_pairs):
                lanes = slice(hp * LANES, (hp + 1) * LANES)
                b2 = bp_scr[rows, lanes]
                q2 = qp_scr[rows, lanes]
                k2 = kp_scr[rows, lanes]
                v2 = vp_scr[rows, hp * dv_pair:(hp + 1) * dv_pair]
                a_off = jnp.zeros((CHUNK, LANES), F32)
                for s in OFF_LEVELS:
                    refs = [c * CHUNK + ((r // s) | 1) * s - 1 for r in range(0, CHUNK, s)]
                    bref = jnp.concatenate(
                        [jnp.broadcast_to(bp_scr[r:r + 1, lanes], (s, LANES)) for r in refs], axis=0)
                    odd = ((ri // s) % 2) == 1
                    qs = jnp.where(odd, q2 * jnp.exp(jnp.minimum(b2 - bref, 0.0)), 0.0)
                    ks = jnp.where(odd, 0.0, k2 * jnp.exp(jnp.minimum(bref - b2, 0.0)))
                    kbd = jnp.concatenate([jnp.where(lane_lo, ks, 0.0), jnp.where(lane_lo, 0.0, ks)], axis=0)
                    off = _nt_dot(qs.astype(BF16), kbd.astype(BF16))
                    valid = odd & ((ci // s) == (ri // s) - 1)
                    a_off = a_off + jnp.where(valid, off, 0.0)
                qe_scr[c, hp] = q2 * jnp.exp(b2)
                aoff_scr[c, hp] = a_off
                b_last = bp_scr[(c + 1) * CHUNK - 1:(c + 1) * CHUNK, lanes]
                kdec = k2 * jnp.exp(jnp.minimum(b_last - b2, 0.0))
                kdec_t = jnp.transpose(jnp.concatenate([kdec, jnp.zeros_like(kdec)], axis=0))
                upd = jnp.dot(kdec_t[:, 0:CHUNK].astype(BF16), v2, preferred_element_type=F32)
                dcol = jnp.transpose(jnp.broadcast_to(jnp.exp(b_last), (LANES, LANES)))
                dcol_scr[c, hp] = dcol
                upd_scr[c, hp] = jnp.where(state_mask, upd, 0.0)

        def gla_late(c):
            rows = slice(c * CHUNK, (c + 1) * CHUNK)
            for hp in range(n_pairs):
                lanes = slice(hp * LANES, (hp + 1) * LANES)
                v2 = vp_scr[rows, hp * dv_pair:(hp + 1) * dv_pair]
                vbd = jnp.concatenate(
                    [jnp.concatenate([v2[:, 0:dv_pair // 2], zero_v], axis=1),
                     jnp.concatenate([zero_v, v2[:, dv_pair // 2:dv_pair]], axis=1)], axis=0)
                a_hp = ad_scr[rows, lanes] + aoff_scr[c, hp]
                lhs = jnp.concatenate([qe_scr[c, hp], a_hp], axis=1).astype(BF16)
                rhs = jnp.concatenate([state[hp].astype(BF16), vbd], axis=0)
                o_scr[rows, hp * dv_pair:(hp + 1) * dv_pair] = jnp.dot(lhs, rhs, preferred_element_type=F32)
                dcol = dcol_scr[c, hp]
                state[hp] = state[hp] * jnp.concatenate([dcol, dcol], axis=1) + upd_scr[c, hp]

        mod = mod_ref[0]
        hb = _rms_mod(x_ref[0], mod[:, 0:d_model], mod[:, d_model:2 * d_model]).astype(BF16)

        def proj(lo, hi):
            return jnp.dot(hb, w_in_ref[:, lo:hi], preferred_element_type=F32)

        assert n_chunks == 8 and dk_all == 2 * LANES and d_pool == 4 * LANES and dv_all == 4 * LANES
        gla_early(0)
        z = jnp.dot(hb, w_z_ref[...], preferred_element_type=F32)
        gla_early(1)
        q_scr[...] = proj(o_q, o_k) * (float(dk_all // GLA_HEADS) ** -0.5)
        gpre = jnp.dot(z.astype(BF16), w_g2_ref[...], preferred_element_type=F32) + b_gate_ref[...]
        glog = (jnp.minimum(gpre, 0.0) - jnp.log1p(jnp.exp(-jnp.abs(gpre)))) * (1.0 / GATE_NORM)
        g_hi = glog.astype(BF16)
        g_lo = (glog - g_hi.astype(F32)).astype(BF16)
        gla_early(2)
        k_scr[...] = proj(o_k, o_v)
        gla_early(3)
        uh_scr[0:POOL_HIST, :] = histp_scr[...]
        uh_scr[POOL_HIST:POOL_HIST + tile, 0:2 * LANES] = proj(0, 2 * LANES)
        diag_blocks()
        gla_early(4)
        gla_late(0)
        uh_scr[POOL_HIST:POOL_HIST + tile, 2 * LANES:d_pool] = proj(2 * LANES, d_pool)
        cs = jnp.dot(tri_ref[...], jnp.concatenate([g_hi, g_lo], axis=1), preferred_element_type=F32)
        b_scr[...] = cs[:, 0:dk_all] + cs[:, dk_all:2 * dk_all]
        gla_early(5)
        gla_late(1)
        v_scr[:, 0:2 * LANES] = proj(o_v, o_v + 2 * LANES).astype(BF16)
        gla_early(6)
        gla_late(2)
        v_scr[:, 2 * LANES:dv_all] = proj(o_v + 2 * LANES, o_r).astype(BF16)
        gla_early(7)
        gla_late(3)

        pos = (t_cur * tile + 1 + lax.broadcasted_iota(jnp.int32, (tile, 1), 0)).astype(F32)
        for gi, w in enumerate(POOL_WINDOWS):
            cols = slice(gi * LANES, (gi + 1) * LANES)
            acc = uh_scr[POOL_HIST:POOL_HIST + tile, cols]
            u_g = acc
            for s in range(1, w):
                acc = acc + uh_scr[POOL_HIST - s:POOL_HIST - s + tile, cols]
            cnt = jnp.minimum(pos, float(w))
            d_g = (acc / cnt - u_g).astype(BF16)
            y_g = jnp.dot(d_g, w_pool_ref[gi], preferred_element_type=F32)
            yp_scr[:, cols] = (y_g * pool_scale_ref[:, cols]).astype(BF16)
        hist_scr[...] = uh_scr[tile:tile + POOL_HIST, :]

        gla_late(4)
        gla_late(5)
        r_scr[:, 0:2 * LANES] = proj(o_r, o_r + 2 * LANES)
        gla_late(6)
        gla_late(7)
        for hp in range(n_pairs):
            s_scr[hp] = state[hp]

        o = o_scr[...]
        ms = jnp.dot((o * o).astype(BF16), hmean_ref[...], preferred_element_type=F32)
        r_scr[:, 2 * LANES:dv_all] = proj(o_r + 2 * LANES, o_r + dv_all)
        y_gla = o * lax.rsqrt(ms + EPS) * gla_norm_ref[...] * _silu(rp_scr[...])
        y = jnp.concatenate([ypp_scr[...], y_gla.astype(BF16)], axis=1)
        g1 = modp_scr[:, 2 * d_model:3 * d_model]
        o_ref[0] = xp_scr[...] + g1 * jnp.dot(y, w_out_ref[...], preferred_element_type=F32)

        xc_scr[...] = x_ref[0]
        modc_scr[...] = mod

    @pl.when(i == 0)
    def _():
        for ref in set1:
            ref[...] = jnp.zeros(ref.shape, ref.dtype)

    @pl.when(t_cur == 0)
    def _():
        hist0_scr[...] = jnp.zeros(hist0_scr.shape, F32)
        hist1_scr[...] = jnp.zeros(hist1_scr.shape, F32)

    @pl.when(t_prev == 0)
    def _():
        s_scr[...] = jnp.zeros(s_scr.shape, F32)

    @pl.when(i % 2 == 0)
    def _():
        step(set0, set1)

    @pl.when(i % 2 == 1)
    def _():
        step(set1, set0)


def _mixer(x, mods, layer, w_in, w_z, w_g2, b_gate, w_pool, pool_scale, gla_norm, w_out, tri, wred, hmean):
    bsz, seq, d_model = x.shape
    d_pool = pool_scale.shape[-1]
    dk_all = b_gate.shape[-1]
    dv_all = gla_norm.shape[-1]
    tile = min(TILE_MIX, seq)
    n_t = seq // tile
    n_pairs = dk_all // LANES
    kern = functools.partial(_mixer_kernel, tile=tile, n_t=n_t, n_tiles=bsz * n_t, d_model=d_model,
                             d_pool=d_pool, dk_all=dk_all, dv_all=dv_all)
    params = (w_in, w_z, w_g2, b_gate, w_pool, pool_scale, gla_norm, w_out)
    consts = (tri, wred, hmean)
    x_cur, x_prev, mod_cur = _tile_specs(bsz, n_t, tile, d_model, mods.shape[-1], layer)

    def stage():
        return [
            pltpu.VMEM((tile, dk_all), F32),
            pltpu.VMEM((tile, dk_all), F32),
            pltpu.VMEM((tile, dk_all), F32),
            pltpu.VMEM((tile, dv_all), BF16),
            pltpu.VMEM((tile, d_pool), BF16),
            pltpu.VMEM((tile, dv_all), F32),
            pltpu.VMEM((tile, d_model), F32),
            pltpu.VMEM((1, mods.shape[-1]), F32),
            pltpu.VMEM((POOL_HIST, d_pool), F32),
        ]

    return pl.pallas_call(
        kern,
        grid=(bsz * n_t + 1,),
        in_specs=[x_cur, mod_cur] + [_layer_spec(a.shape, layer) for a in params]
                 + [_const_spec(a.shape) for a in consts],
        out_specs=x_prev,
        out_shape=jax.ShapeDtypeStruct(x.shape, F32),
        scratch_shapes=[
            pltpu.VMEM((POOL_HIST + tile, d_pool), F32),
            pltpu.VMEM((n_pairs, LANES, dv_all // n_pairs), F32),
            pltpu.VMEM((tile, dk_all), F32),
            pltpu.VMEM((tile, dv_all), F32),
            pltpu.VMEM((n_pairs, tile, SUB * LANES), BF16),
            pltpu.VMEM((tile // CHUNK, n_pairs, CHUNK, LANES), F32),
            pltpu.VMEM((tile // CHUNK, n_pairs, CHUNK, LANES), F32),
            pltpu.VMEM((tile // CHUNK, n_pairs, LANES, LANES), F32),
            pltpu.VMEM((tile // CHUNK, n_pairs, LANES, 2 * LANES), F32),
        ] + stage() + stage(),
        compiler_params=pltpu.CompilerParams(dimension_semantics=("arbitrary",),
                                             vmem_limit_bytes=VMEM_LIMIT),
        name="token_mixer",
    )(x, mods, *params, *consts)


def _ffn_kernel(x_ref, mod_ref, w_up_ref, cw_ref, cb_ref, w_dn_ref, fn_ref, o_ref,
                carry_scr, act_scr, *, tile, n_t, d_model, d_ff, final):
    i = pl.program_id(0)

    @pl.when(i % n_t == 0)
    def _():
        carry_scr[...] = jnp.zeros(carry_scr.shape, F32)

    x = x_ref[0]
    mod = mod_ref[0]
    sh2 = mod[:, 3 * d_model:4 * d_model]
    sc2 = mod[:, 4 * d_model:5 * d_model]
    g2 = mod[:, 5 * d_model:6 * d_model]
    hb = _rms_mod(x, sh2, sc2).astype(BF16)

    row8 = lax.broadcasted_iota(jnp.int32, (8, FF_COLS), 0)
    for c in range(d_ff // FF_COLS):
        halves = []
        for half in range(2):
            cols = slice(half * d_ff + c * FF_COLS, half * d_ff + (c + 1) * FF_COLS)
            up = jnp.dot(hb, w_up_ref[:, cols], preferred_element_type=F32)
            prev = carry_scr[:, cols]
            carry_scr[:, cols] = up[tile - 8:tile, :]
            r1 = pltpu.roll(up, 1, axis=0)
            r2 = pltpu.roll(up, 2, axis=0)
            h1 = jnp.where(row8 < 1, pltpu.roll(prev, 1, axis=0), r1[0:8, :])
            h2 = jnp.where(row8 < 2, pltpu.roll(prev, 2, axis=0), r2[0:8, :])
            up1 = jnp.concatenate([h1, r1[8:, :]], axis=0)
            up2 = jnp.concatenate([h2, r2[8:, :]], axis=0)
            halves.append(cw_ref[2:3, cols] * up + cw_ref[1:2, cols] * up1
                          + cw_ref[0:1, cols] * up2 + cb_ref[:, cols])
        act_scr[:, c * FF_COLS:(c + 1) * FF_COLS] = (_silu(halves[0]) * halves[1]).astype(BF16)
    out = x + g2 * jnp.dot(act_scr[...], w_dn_ref[...], preferred_element_type=F32)
    if final:
        ms = jnp.mean(out * out, axis=-1, keepdims=True)
        out = out * lax.rsqrt(ms + EPS) * fn_ref[...]
    o_ref[0] = out


def _ffn(x, mods, layer, w_up, conv_w, conv_b, w_down, final_norm, *, final):
    bsz, seq, d_model = x.shape
    d_ff = w_down.shape[1]
    tile = min(TILE_FFN, seq)
    n_t = seq // tile
    kern = functools.partial(_ffn_kernel, tile=tile, n_t=n_t, d_model=d_model, d_ff=d_ff, final=final)
    params = (w_up, conv_w, conv_b, w_down)
    x_spec = pl.BlockSpec((1, tile, d_model), lambda i: (i // n_t, i % n_t, 0))
    mod_spec = pl.BlockSpec((None, 1, 1, mods.shape[-1]), lambda i: (layer, i // n_t, 0, 0))
    return pl.pallas_call(
        kern,
        grid=(bsz * n_t,),
        in_specs=[x_spec, mod_spec] + [_layer_spec(a.shape, layer) for a in params]
                 + [_const_spec(final_norm.shape)],
        out_specs=x_spec,
        out_shape=jax.ShapeDtypeStruct(x.shape, F32),
        scratch_shapes=[
            pltpu.VMEM((8, 2 * d_ff), F32),
            pltpu.VMEM((tile, d_ff), BF16),
        ],
        compiler_params=pltpu.CompilerParams(dimension_semantics=("arbitrary",),
                                             vmem_limit_bytes=VMEM_LIMIT),
        name="conv_ffn",
    )(x, mods, *params, final_norm)


def _mixer_constants(tile, dk_all, dv_all):
    r = np.arange(tile)
    tri = ((r[:, None] // CHUNK == r[None, :] // CHUNK) & (r[:, None] >= r[None, :]))
    kk = np.arange(SUB * LANES)
    wred = (kk[:, None] // LANES + (LANES // 2) * ((kk[:, None] % LANES) // (LANES // 2))
            == np.arange(LANES)[None, :])
    dv = dv_all // GLA_HEADS
    e = np.arange(dv_all)
    hmean = (e[:, None] // dv == e[None, :] // dv) / float(dv)
    return (jnp.asarray(tri, BF16), jnp.asarray(wred, BF16), jnp.asarray(hmean, BF16))


def kernel(x, c, ada_w, ada_b, w_in, w_gate2, b_gate, w_pool, pool_scale, gla_norm, w_out, w_up,
           conv_w, conv_b, w_down, final_norm):
    depth = ada_w.shape[0]
    seq = x.shape[1]
    d_pool = pool_scale.shape[-1]
    dk_all = b_gate.shape[-1]
    dv_all = gla_norm.shape[-1]
    n_main = d_pool + 2 * dk_all + 2 * dv_all
    assert w_in.shape[-1] == n_main + GATE_RANK and dk_all == 2 * LANES and seq % CHUNK == 0

    mods = _modulation(c, ada_w, ada_b)[:, :, None, :]
    tri, wred, hmean = _mixer_constants(min(TILE_MIX, seq), dk_all, dv_all)
    pad_z = LANES - GATE_RANK
    w_in_b = w_in.astype(BF16)
    w_z = jnp.pad(w_in[:, :, n_main:], ((0, 0), (0, 0), (0, pad_z))).astype(BF16)
    w_g2 = jnp.pad(w_gate2, ((0, 0), (0, pad_z), (0, 0))).astype(BF16)
    w_pool_b, w_out_b, w_up_b, w_down_b = (w.astype(BF16) for w in (w_pool, w_out, w_up, w_down))

    def rows(a):
        return a[:, None, :]

    for l in range(depth):
        x = _mixer(x, mods, l, w_in_b, w_z, w_g2, rows(b_gate), w_pool_b, rows(pool_scale),
                   rows(gla_norm), w_out_b, tri, wred, hmean)
        x = _ffn(x, mods, l, w_up_b, conv_w, rows(conv_b), w_down_b, final_norm[None, :],
                 final=(l == depth - 1))
    return x
```

```python
import functools

import jax
import jax.numpy as jnp
import numpy as np
from jax import lax
from jax.experimental import pallas as pl
from jax.experimental.pallas import tpu as pltpu

F32 = jnp.float32
BF16 = jnp.bfloat16

EPS = 1e-6
POOL_WINDOWS = (2, 4, 8, 16)
POOL_HIST = 16
GLA_HEADS = 4
GATE_RANK = 16
GATE_NORM = 16.0
CHUNK = 64
SUB = 8
OFF_LEVELS = (32, 16, 8)
CONV_W = 3
LANES = 128
VMEM_LIMIT = 56 * 1024 * 1024

TILE_MIX = 512
TILE_FFN = 1024
FF_COLS = 256
MOD_COLS = 1024


def _silu(x):
    h = 0.5 * x
    return h + h * jnp.tanh(h)


def _rms_mod(x, shift, scale):
    ms = jnp.mean(x * x, axis=-1, keepdims=True)
    return x * lax.rsqrt(ms + EPS) * (1.0 + scale) + shift


def _nt_dot(a, b):
    return lax.dot_general(a, b, (((1,), (1,)), ((), ())), preferred_element_type=F32)


def _mod_kernel(c_ref, w_ref, b_ref, o_ref):
    ca = _silu(c_ref[...])
    c_hi = ca.astype(BF16)
    c_lo = (ca - c_hi.astype(F32)).astype(BF16)
    w = w_ref[0]
    w_hi = w.astype(BF16)
    w_lo = (w - w_hi.astype(F32)).astype(BF16)
    bsz = ca.shape[0]
    main = jnp.dot(jnp.concatenate([c_hi, c_lo], axis=0), w_hi, preferred_element_type=F32)
    o_ref[0] = (main[0:bsz] + main[bsz:2 * bsz]
                + jnp.dot(c_hi, w_lo, preferred_element_type=F32) + b_ref[0])


def _modulation(c, ada_w, ada_b):
    n_layers, d, n = ada_w.shape
    bsz = c.shape[0]
    return pl.pallas_call(
        _mod_kernel,
        grid=(n_layers, n // MOD_COLS),
        in_specs=[
            pl.BlockSpec((bsz, d), lambda l, j: (0, 0)),
            pl.BlockSpec((1, d, MOD_COLS), lambda l, j: (l, 0, j)),
            pl.BlockSpec((1, 1, MOD_COLS), lambda l, j: (l, 0, j)),
        ],
        out_specs=pl.BlockSpec((1, bsz, MOD_COLS), lambda l, j: (l, 0, j)),
        out_shape=jax.ShapeDtypeStruct((n_layers, bsz, n), F32),
        compiler_params=pltpu.CompilerParams(dimension_semantics=("arbitrary", "arbitrary")),
        name="adaln_mod",
    )(c, ada_w, ada_b.reshape(n_layers, 1, n))


def _tile_specs(bsz, n_t, tile, d_model, mod_width, layer):
    last = bsz * n_t - 1

    def cur(i):
        return jnp.minimum(i, last)

    def prev(i):
        return jnp.maximum(i - 1, 0)

    x_cur = pl.BlockSpec((1, tile, d_model), lambda i: (cur(i) // n_t, cur(i) % n_t, 0))
    x_prev = pl.BlockSpec((1, tile, d_model), lambda i: (prev(i) // n_t, prev(i) % n_t, 0))
    mod_cur = pl.BlockSpec((None, 1, 1, mod_width), lambda i: (layer, cur(i) // n_t, 0, 0))
    return x_cur, x_prev, mod_cur


def _const_spec(shape):
    zeros = (0,) * len(shape)
    return pl.BlockSpec(shape, lambda i: zeros, pipeline_mode=pl.Buffered(1))


def _layer_spec(shape, layer):
    idx = (layer,) + (0,) * (len(shape) - 1)
    return pl.BlockSpec((None,) + tuple(shape[1:]), lambda i: idx, pipeline_mode=pl.Buffered(1))


def _mixer_kernel(x_ref, mod_ref, w_in_ref, w_z_ref, w_g2_ref, b_gate_ref,
                  w_pool_ref, pool_scale_ref, gla_norm_ref, w_out_ref, tri_ref, wred_ref, hmean_ref,
                  o_ref,
                  uh_scr, s_scr, ad_scr, o_scr, pc_scr, qe_scr, aoff_scr, dcol_scr, upd_scr,
                  b0_scr, q0_scr, k0_scr, v0_scr, yp0_scr, r0_scr, x0_scr, mod0_scr, hist0_scr,
                  b1_scr, q1_scr, k1_scr, v1_scr, yp1_scr, r1_scr, x1_scr, mod1_scr, hist1_scr,
                  *, tile, n_t, n_tiles, d_model, d_pool, dk_all, dv_all):
    i = pl.program_id(0)
    t_cur = jnp.minimum(i, n_tiles - 1) % n_t
    t_prev = jnp.maximum(i - 1, 0) % n_t
    n_chunks = tile // CHUNK
    n_pairs = dk_all // LANES
    dv_pair = dv_all // n_pairs
    set0 = (b0_scr, q0_scr, k0_scr, v0_scr, yp0_scr, r0_scr, x0_scr, mod0_scr, hist0_scr)
    set1 = (b1_scr, q1_scr, k1_scr, v1_scr, yp1_scr, r1_scr, x1_scr, mod1_scr, hist1_scr)

    def step(cur, prv):
        b_scr, q_scr, k_scr, v_scr, yp_scr, r_scr, xc_scr, modc_scr, hist_scr = cur
        bp_scr, qp_scr, kp_scr, vp_scr, ypp_scr, rp_scr, xp_scr, modp_scr, histp_scr = prv
        o_q = d_pool
        o_k = o_q + dk_all
        o_v = o_k + dk_all
        o_r = o_v + dv_all

        for c in range(n_chunks):
            for hp in range(n_pairs):
                lanes = slice(hp * LANES, (hp + 1) * LANES)
                for blk2 in range(CHUNK // (2 * SUB)):
                    rows = []
                    for half in range(2):
                        r0 = c * CHUNK + (2 * blk2 + half) * SUB
                        b8 = bp_scr[r0:r0 + SUB, lanes]
                        q8 = qp_scr[r0:r0 + SUB, lanes]
                        pj = []
                        for j in range(SUB):
                            bj = bp_scr[r0 + j:r0 + j + 1, lanes]
                            kj = kp_scr[r0 + j:r0 + j + 1, lanes]
                            e = jnp.exp(jnp.minimum(b8 - bj, 0.0))
                            pj.append(q8 * e * kj)
                        rows.append(jnp.concatenate(pj, axis=1))
                    blk = jnp.concatenate(rows, axis=0).astype(BF16)
                    r0 = c * CHUNK + blk2 * 2 * SUB
                    pc_scr[hp, r0:r0 + 2 * SUB, :] = blk

        def diag_blocks():
            row_l = lax.broadcasted_iota(jnp.int32, (SUB, LANES), 0)
            col_l = lax.broadcasted_iota(jnp.int32, (SUB, LANES), 1) % (LANES // 2)
            tril8 = row_l >= col_l
            for hp in range(n_pairs):
                ad = jnp.dot(pc_scr[hp], wred_ref[...], preferred_element_type=F32)
                placed = []
                for rb in range(tile // SUB):
                    a8 = jnp.where(tril8, ad[rb * SUB:(rb + 1) * SUB, :], 0.0)
                    shift = (rb % (CHUNK // SUB)) * SUB
                    placed.append(pltpu.roll(a8, shift, axis=1) if shift else a8)
                ad_scr[:, hp * LANES:(hp + 1) * LANES] = jnp.concatenate(placed, axis=0)

        ri = lax.broadcasted_iota(jnp.int32, (CHUNK, LANES), 0)
        ci = lax.broadcasted_iota(jnp.int32, (CHUNK, LANES), 1) % (LANES // 2)
        lane_lo = lax.broadcasted_iota(jnp.int32, (CHUNK, LANES), 1) < (LANES // 2)
        r2 = lax.broadcasted_iota(jnp.int32, (LANES, 2 * LANES), 0)
        c2 = lax.broadcasted_iota(jnp.int32, (LANES, 2 * LANES), 1)
        state_mask = (r2 < LANES // 2) == (c2 < LANES)
        zero_v = jnp.zeros((CHUNK, dv_pair // 2), BF16)
        state = [s_scr[hp] for hp in range(n_pairs)]

        def gla_early(c):
            rows = slice(c * CHUNK, (c + 1) * CHUNK)
            for hp in range(n_pairs):
                lanes = slice(hp * LANES, (hp + 1) * LANES)
                b2 = bp_scr[rows, lanes]
                q2 = qp_scr[rows, lanes]
                k2 = kp_scr[rows, lanes]
                v2 = vp_scr[rows, hp * dv_pair:(hp + 1) * dv_pair]
                a_off = jnp.zeros((CHUNK, LANES), F32)
                for s in OFF_LEVELS:
                    refs = [c * CHUNK + ((r // s) | 1) * s - 1 for r in range(0, CHUNK, s)]
                    bref = jnp.concatenate(
                        [jnp.broadcast_to(bp_scr[r:r + 1, lanes], (s, LANES)) for r in refs], axis=0)
                    odd = ((ri // s) % 2) == 1
                    qs = jnp.where(odd, q2 * jnp.exp(jnp.minimum(b2 - bref, 0.0)), 0.0)
                    ks = jnp.where(odd, 0.0, k2 * jnp.exp(jnp.minimum(bref - b2, 0.0)))
                    kbd = jnp.concatenate([jnp.where(lane_lo, ks, 0.0), jnp.where(lane_lo, 0.0, ks)], axis=0)
                    off = _nt_dot(qs.astype(BF16), kbd.astype(BF16))
                    valid = odd & ((ci // s) == (ri // s) - 1)
                    a_off = a_off + jnp.where(valid, off, 0.0)
                qe_scr[c, hp] = q2 * jnp.exp(b2)
                aoff_scr[c, hp] = a_off
                b_last = bp_scr[(c + 1) * CHUNK - 1:(c + 1) * CHUNK, lanes]
                kdec = k2 * jnp.exp(jnp.minimum(b_last - b2, 0.0))
                kdec_t = jnp.transpose(jnp.concatenate([kdec, jnp.zeros_like(kdec)], axis=0))
                upd = jnp.dot(kdec_t[:, 0:CHUNK].astype(BF16), v2, preferred_element_type=F32)
                dcol = jnp.transpose(jnp.broadcast_to(jnp.exp(b_last), (LANES, LANES)))
                dcol_scr[c, hp] = dcol
                upd_scr[c, hp] = jnp.where(state_mask, upd, 0.0)

        def gla_late(c):
            rows = slice(c * CHUNK, (c + 1) * CHUNK)
            for hp in range(n_pairs):
                lanes = slice(hp * LANES, (hp + 1) * LANES)
                v2 = vp_scr[rows, hp * dv_pair:(hp + 1) * dv_pair]
                vbd = jnp.concatenate(
                    [jnp.concatenate([v2[:, 0:dv_pair // 2], zero_v], axis=1),
                     jnp.concatenate([zero_v, v2[:, dv_pair // 2:dv_pair]], axis=1)], axis=0)
                a_hp = ad_scr[rows, lanes] + aoff_scr[c, hp]
                lhs = jnp.concatenate([qe_scr[c, hp], a_hp], axis=1).astype(BF16)
                rhs = jnp.concatenate([state[hp].astype(BF16), vbd], axis=0)
                o_scr[rows, hp * dv_pair:(hp + 1) * dv_pair] = jnp.dot(lhs, rhs, preferred_element_type=F32)
                dcol = dcol_scr[c, hp]
                state[hp] = state[hp] * jnp.concatenate([dcol, dcol], axis=1) + upd_scr[c, hp]

        mod = mod_ref[0]
        hb = _rms_mod(x_ref[0], mod[:, 0:d_model], mod[:, d_model:2 * d_model]).astype(BF16)

        def proj(lo, hi):
            return jnp.dot(hb, w_in_ref[:, lo:hi], preferred_element_type=F32)

        assert n_chunks == 8 and dk_all == 2 * LANES and d_pool == 4 * LANES and dv_all == 4 * LANES
        gla_early(0)
        z = jnp.dot(hb, w_z_ref[...], preferred_element_type=F32)
        gla_early(1)
        q_scr[...] = proj(o_q, o_k) * (float(dk_all // GLA_HEADS) ** -0.5)
        gpre = jnp.dot(z.astype(BF16), w_g2_ref[...], preferred_element_type=F32) + b_gate_ref[...]
        glog = (jnp.minimum(gpre, 0.0) - jnp.log1p(jnp.exp(-jnp.abs(gpre)))) * (1.0 / GATE_NORM)
        g_hi = glog.astype(BF16)
        g_lo = (glog - g_hi.astype(F32)).astype(BF16)
        gla_early(2)
        k_scr[...] = proj(o_k, o_v)
        gla_early(3)
        uh_scr[0:POOL_HIST, :] = histp_scr[...]
        uh_scr[POOL_HIST:POOL_HIST + tile, 0:2 * LANES] = proj(0, 2 * LANES)
        diag_blocks()
        gla_early(4)
        gla_late(0)
        uh_scr[POOL_HIST:POOL_HIST + tile, 2 * LANES:d_pool] = proj(2 * LANES, d_pool)
        cs = jnp.dot(tri_ref[...], jnp.concatenate([g_hi, g_lo], axis=1), preferred_element_type=F32)
        b_scr[...] = cs[:, 0:dk_all] + cs[:, dk_all:2 * dk_all]
        gla_early(5)
        gla_late(1)
        v_scr[:, 0:2 * LANES] = proj(o_v, o_v + 2 * LANES).astype(BF16)
        gla_early(6)
        gla_late(2)
        v_scr[:, 2 * LANES:dv_all] = proj(o_v + 2 * LANES, o_r).astype(BF16)
        gla_early(7)
        gla_late(3)

        pos = (t_cur * tile + 1 + lax.broadcasted_iota(jnp.int32, (tile, 1), 0)).astype(F32)
        for gi, w in enumerate(POOL_WINDOWS):
            cols = slice(gi * LANES, (gi + 1) * LANES)
            acc = uh_scr[POOL_HIST:POOL_HIST + tile, cols]
            u_g = acc
            for s in range(1, w):
                acc = acc + uh_scr[POOL_HIST - s:POOL_HIST - s + tile, cols]
            cnt = jnp.minimum(pos, float(w))
            d_g = (acc / cnt - u_g).astype(BF16)
            y_g = jnp.dot(d_g, w_pool_ref[gi], preferred_element_type=F32)
            yp_scr[:, cols] = (y_g * pool_scale_ref[:, cols]).astype(BF16)
        hist_scr[...] = uh_scr[tile:tile + POOL_HIST, :]

        gla_late(4)
        gla_late(5)
        r_scr[:, 0:2 * LANES] = proj(o_r, o_r + 2 * LANES)
        gla_late(6)
        gla_late(7)
        for hp in range(n_pairs):
            s_scr[hp] = state[hp]

        r_scr[:, 2 * LANES:dv_all] = proj(o_r + 2 * LANES, o_r + dv_all)
        dv = dv_all // GLA_HEADS
        o_n = []
        for h in range(GLA_HEADS):
            o_h = o_scr[:, h * dv:(h + 1) * dv]
            o_n.append(o_h * lax.rsqrt(jnp.mean(o_h * o_h, axis=-1, keepdims=True) + EPS))
        y_gla = jnp.concatenate(o_n, axis=1) * gla_norm_ref[...] * _silu(rp_scr[...])
        y = jnp.concatenate([ypp_scr[...], y_gla.astype(BF16)], axis=1)
        g1 = modp_scr[:, 2 * d_model:3 * d_model]
        o_ref[0] = xp_scr[...] + g1 * jnp.dot(y, w_out_ref[...], preferred_element_type=F32)

        xc_scr[...] = x_ref[0]
        modc_scr[...] = mod

    @pl.when(i == 0)
    def _():
        for ref in set1:
            ref[...] = jnp.zeros(ref.shape, ref.dtype)

    @pl.when(t_cur == 0)
    def _():
        hist0_scr[...] = jnp.zeros(hist0_scr.shape, F32)
        hist1_scr[...] = jnp.zeros(hist1_scr.shape, F32)

    @pl.when(t_prev == 0)
    def _():
        s_scr[...] = jnp.zeros(s_scr.shape, F32)

    @pl.when(i % 2 == 0)
    def _():
        step(set0, set1)

    @pl.when(i % 2 == 1)
    def _():
        step(set1, set0)


def _mixer(x, mods, layer, w_in, w_z, w_g2, b_gate, w_pool, pool_scale, gla_norm, w_out, tri, wred, hmean):
    bsz, seq, d_model = x.shape
    d_pool = pool_scale.shape[-1]
    dk_all = b_gate.shape[-1]
    dv_all = gla_norm.shape[-1]
    tile = min(TILE_MIX, seq)
    n_t = seq // tile
    n_pairs = dk_all // LANES
    kern = functools.partial(_mixer_kernel, tile=tile, n_t=n_t, n_tiles=bsz * n_t, d_model=d_model,
                             d_pool=d_pool, dk_all=dk_all, dv_all=dv_all)
    params = (w_in, w_z, w_g2, b_gate, w_pool, pool_scale, gla_norm, w_out)
    consts = (tri, wred, hmean)
    x_cur, x_prev, mod_cur = _tile_specs(bsz, n_t, tile, d_model, mods.shape[-1], layer)

    def stage():
        return [
            pltpu.VMEM((tile, dk_all), F32),
            pltpu.VMEM((tile, dk_all), F32),
            pltpu.VMEM((tile, dk_all), F32),
            pltpu.VMEM((tile, dv_all), BF16),
            pltpu.VMEM((tile, d_pool), BF16),
            pltpu.VMEM((tile, dv_all), F32),
            pltpu.VMEM((tile, d_model), F32),
            pltpu.VMEM((1, mods.shape[-1]), F32),
            pltpu.VMEM((POOL_HIST, d_pool), F32),
        ]

    return pl.pallas_call(
        kern,
        grid=(bsz * n_t + 1,),
        in_specs=[x_cur, mod_cur] + [_layer_spec(a.shape, layer) for a in params]
                 + [_const_spec(a.shape) for a in consts],
        out_specs=x_prev,
        out_shape=jax.ShapeDtypeStruct(x.shape, F32),
        scratch_shapes=[
            pltpu.VMEM((POOL_HIST + tile, d_pool), F32),
            pltpu.VMEM((n_pairs, LANES, dv_all // n_pairs), F32),
            pltpu.VMEM((tile, dk_all), F32),
            pltpu.VMEM((tile, dv_all), F32),
            pltpu.VMEM((n_pairs, tile, SUB * LANES), BF16),
            pltpu.VMEM((tile // CHUNK, n_pairs, CHUNK, LANES), F32),
            pltpu.VMEM((tile // CHUNK, n_pairs, CHUNK, LANES), F32),
            pltpu.VMEM((tile // CHUNK, n_pairs, LANES, LANES), F32),
            pltpu.VMEM((tile // CHUNK, n_pairs, LANES, 2 * LANES), F32),
        ] + stage() + stage(),
        compiler_params=pltpu.CompilerParams(dimension_semantics=("arbitrary",),
                                             vmem_limit_bytes=VMEM_LIMIT),
        name="token_mixer",
    )(x, mods, *params, *consts)


def _ffn_kernel(x_ref, mod_ref, w_up_ref, cw_ref, cb_ref, w_dn_ref, fn_ref, o_ref,
                carry_scr, act_scr, *, tile, n_t, d_model, d_ff, final):
    i = pl.program_id(0)

    @pl.when(i % n_t == 0)
    def _():
        carry_scr[...] = jnp.zeros(carry_scr.shape, F32)

    x = x_ref[0]
    mod = mod_ref[0]
    sh2 = mod[:, 3 * d_model:4 * d_model]
    sc2 = mod[:, 4 * d_model:5 * d_model]
    g2 = mod[:, 5 * d_model:6 * d_model]
    hb = _rms_mod(x, sh2, sc2).astype(BF16)

    row8 = lax.broadcasted_iota(jnp.int32, (8, FF_COLS), 0)
    for c in range(d_ff // FF_COLS):
        halves = []
        for half in range(2):
            cols = slice(half * d_ff + c * FF_COLS, half * d_ff + (c + 1) * FF_COLS)
            up = jnp.dot(hb, w_up_ref[:, cols], preferred_element_type=F32)
            prev = carry_scr[:, cols]
            carry_scr[:, cols] = up[tile - 8:tile, :]
            r1 = pltpu.roll(up, 1, axis=0)
            r2 = pltpu.roll(up, 2, axis=0)
            h1 = jnp.where(row8 < 1, pltpu.roll(prev, 1, axis=0), r1[0:8, :])
            h2 = jnp.where(row8 < 2, pltpu.roll(prev, 2, axis=0), r2[0:8, :])
            up1 = jnp.concatenate([h1, r1[8:, :]], axis=0)
            up2 = jnp.concatenate([h2, r2[8:, :]], axis=0)
            halves.append(cw_ref[2:3, cols] * up + cw_ref[1:2, cols] * up1
                          + cw_ref[0:1, cols] * up2 + cb_ref[:, cols])
        act_scr[:, c * FF_COLS:(c + 1) * FF_COLS] = (_silu(halves[0]) * halves[1]).astype(BF16)
    out = x + g2 * jnp.dot(act_scr[...], w_dn_ref[...], preferred_element_type=F32)
    if final:
        ms = jnp.mean(out * out, axis=-1, keepdims=True)
        out = out * lax.rsqrt(ms + EPS) * fn_ref[...]
    o_ref[0] = out


def _ffn(x, mods, layer, w_up, conv_w, conv_b, w_down, final_norm, *, final):
    bsz, seq, d_model = x.shape
    d_ff = w_down.shape[1]
    tile = min(TILE_FFN, seq)
    n_t = seq // tile
    kern = functools.partial(_ffn_kernel, tile=tile, n_t=n_t, d_model=d_model, d_ff=d_ff, final=final)
    params = (w_up, conv_w, conv_b, w_down)
    x_spec = pl.BlockSpec((1, tile, d_model), lambda i: (i // n_t, i % n_t, 0))
    mod_spec = pl.BlockSpec((None, 1, 1, mods.shape[-1]), lambda i: (layer, i // n_t, 0, 0))
    return pl.pallas_call(
        kern,
        grid=(bsz * n_t,),
        in_specs=[x_spec, mod_spec] + [_layer_spec(a.shape, layer) for a in params]
                 + [_const_spec(final_norm.shape)],
        out_specs=x_spec,
        out_shape=jax.ShapeDtypeStruct(x.shape, F32),
        scratch_shapes=[
            pltpu.VMEM((8, 2 * d_ff), F32),
            pltpu.VMEM((tile, d_ff), BF16),
        ],
        compiler_params=pltpu.CompilerParams(dimension_semantics=("arbitrary",),
                                             vmem_limit_bytes=VMEM_LIMIT),
        name="conv_ffn",
    )(x, mods, *params, final_norm)


def _mixer_constants(tile, dk_all, dv_all):
    r = np.arange(tile)
    tri = ((r[:, None] // CHUNK == r[None, :] // CHUNK) & (r[:, None] >= r[None, :]))
    kk = np.arange(SUB * LANES)
    wred = (kk[:, None] // LANES + (LANES // 2) * ((kk[:, None] % LANES) // (LANES // 2))
            == np.arange(LANES)[None, :])
    dv = dv_all // GLA_HEADS
    e = np.arange(dv_all)
    hmean = (e[:, None] // dv == e[None, :] // dv) / float(dv)
    return (jnp.asarray(tri, BF16), jnp.asarray(wred, BF16), jnp.asarray(hmean, BF16))


def kernel(x, c, ada_w, ada_b, w_in, w_gate2, b_gate, w_pool, pool_scale, gla_norm, w_out, w_up,
           conv_w, conv_b, w_down, final_norm):
    depth = ada_w.shape[0]
    seq = x.shape[1]
    d_pool = pool_scale.shape[-1]
    dk_all = b_gate.shape[-1]
    dv_all = gla_norm.shape[-1]
    n_main = d_pool + 2 * dk_all + 2 * dv_all
    assert w_in.shape[-1] == n_main + GATE_RANK and dk_all == 2 * LANES and seq % CHUNK == 0

    mods = _modulation(c, ada_w, ada_b)[:, :, None, :]
    tri, wred, hmean = _mixer_constants(min(TILE_MIX, seq), dk_all, dv_all)
    pad_z = LANES - GATE_RANK
    w_in_b = w_in.astype(BF16)
    w_z = jnp.pad(w_in[:, :, n_main:], ((0, 0), (0, 0), (0, pad_z))).astype(BF16)
    w_g2 = jnp.pad(w_gate2, ((0, 0), (0, pad_z), (0, 0))).astype(BF16)
    w_pool_b, w_out_b, w_up_b, w_down_b = (w.astype(BF16) for w in (w_pool, w_out, w_up, w_down))

    def rows(a):
        return a[:, None, :]

    for l in range(depth):
        x = _mixer(x, mods, l, w_in_b, w_z, w_g2, rows(b_gate), w_pool_b, rows(pool_scale),
                   rows(gla_norm), w_out_b, tri, wred, hmean)
        x = _ffn(x, mods, l, w_up_b, conv_w, rows(conv_b), w_down_b, final_norm[None, :],
                 final=(l == depth - 1))
    return x
```
